```python
import math
import jax, jax.numpy as jnp
from jax import lax
import numpy as np

D_MODEL = 1024
BATCH = 8
SEQ = 2048
DEPTH = 2

N_A_LAYERS = DEPTH // 2
N_B_LAYERS = DEPTH - N_A_LAYERS
HEAD_DIM = 64
RWKV_HEADS = D_MODEL // HEAD_DIM
DECAY_LORA = 64
AAA_LORA = 64
GATE_LORA = 160
ATTN_HEADS = D_MODEL // HEAD_DIM
KV_HEADS = 4
GROUP = ATTN_HEADS // KV_HEADS
WINDOW = 128
BLOCK = 128
D_FF = 2816
CONV_WIDTH = 3
RMS_EPS = 1e-6
GN_EPS = 64e-5
NEG_INF = -1e30

kernel_name = 'yoco_rwkv7_swa_sink_convffn_trunk'


def rms_norm(x, g):
    xf = x.astype(jnp.float32)
    y = xf * lax.rsqrt(jnp.mean(xf * xf, axis=-1, keepdims=True) + RMS_EPS)
    return (y * g.astype(jnp.float32)).astype(x.dtype)


def conv_ffn(x, w_up, conv_w, conv_b, w_down):
    h = x @ w_up
    c = h.shape[-1]
    h = lax.conv_general_dilated(
        h, conv_w[:, None, :].astype(h.dtype), window_strides=(1,),
        padding=[(CONV_WIDTH - 1, 0)], dimension_numbers=('NWC', 'WIO', 'NWC'),
        feature_group_count=c) + conv_b
    a, u = jnp.split(h, 2, axis=-1)
    return (jax.nn.gelu(a, approximate=True) * u) @ w_down


def rwkv7_time_mix(x, mu, w_rkv, w_decay0, w_decay1, w_decay2, a0, a1, a2, g1, g2,
                   k_k, k_a, r_k, gn_g, gn_b, w_o):
    B, T, D = x.shape
    H, N = RWKV_HEADS, HEAD_DIM
    f32 = jnp.float32
    xx = jnp.pad(x, ((0, 0), (1, 0), (0, 0)))[:, :-1] - x
    xr, xw, xk, xv, xa, xg = [x + xx * mu[i] for i in range(6)]
    r = xr @ w_rkv[0]
    k = xk @ w_rkv[1]
    v = xv @ w_rkv[2]
    w = -jax.nn.softplus(-(w_decay0 + jnp.tanh(xw @ w_decay1) @ w_decay2)) - 0.5
    a = jax.nn.sigmoid(a0 + (xa @ a1) @ a2)
    g = jax.nn.sigmoid(xg @ g1) @ g2
    heads = lambda z: z.reshape(B, T, H, N).astype(f32)
    kk = heads(k * k_k)
    kk = kk * lax.rsqrt(jnp.maximum(jnp.sum(kk * kk, axis=-1, keepdims=True), 1e-24))
    k = k * (1 + (a - 1) * k_a)
    r_h, k_h, v_h, a_h = heads(r), heads(k), heads(v), heads(a)
    decay = jnp.exp(-jnp.exp(heads(w)))
    vec_a, vec_b = -kk, kk * a_h

    def step(S, inp):
        r_t, d_t, k_t, v_t, a_t, b_t = inp
        sa = jnp.einsum('bhij,bhj->bhi', S, a_t)
        S = (S * d_t[:, :, None, :] + sa[..., None] * b_t[:, :, None, :]
             + v_t[..., None] * k_t[:, :, None, :])
        return S, jnp.einsum('bhij,bhj->bhi', S, r_t)

    seq_first = lambda z: jnp.moveaxis(z, 1, 0)
    state0 = jnp.zeros((B, H, N, N), f32)
    _, y = lax.scan(step, state0, tuple(seq_first(z) for z in (r_h, decay, k_h, v_h, vec_a, vec_b)))
    y = jnp.moveaxis(y, 0, 1)
    mean = jnp.mean(y, axis=-1, keepdims=True)
    var = jnp.mean(jnp.square(y - mean), axis=-1, keepdims=True)
    y = ((y - mean) * lax.rsqrt(var + GN_EPS)).reshape(B, T, D)
    y = y * gn_g.astype(f32) + gn_b.astype(f32)
    bonus = jnp.sum(r_h * k_h * r_k.astype(f32), axis=-1, keepdims=True) * v_h
    y = y + bonus.reshape(B, T, D)
    return (y * g.astype(f32)).astype(x.dtype) @ w_o


def shared_kv(h, kv_g, w_kv):
    B, T, _ = h.shape
    kv = rms_norm(h, kv_g) @ w_kv
    k, v = jnp.split(kv, 2, axis=-1)
    return k.reshape(B, T, KV_HEADS, HEAD_DIM), v.reshape(B, T, KV_HEADS, HEAD_DIM)


def swa_sink_attention(x, k, v, w_q, sinks, w_o):
    B, T, D = x.shape
    nb = T // BLOCK
    q = (x @ w_q).reshape(B, nb, BLOCK, KV_HEADS, GROUP, HEAD_DIM)

    def band(z):
        zp = jnp.pad(z, ((0, 0), (BLOCK, 0), (0, 0), (0, 0))).reshape(B, nb + 1, BLOCK, KV_HEADS, HEAD_DIM)
        return jnp.concatenate([zp[:, :-1], zp[:, 1:]], axis=2)

    kb, vb = band(k), band(v)
    s = jnp.einsum('bnqhgd,bnkhd->bnhgqk', q, kb).astype(jnp.float32) * (HEAD_DIM ** -0.5)
    q_pos = jnp.arange(BLOCK)[:, None]
    k_pos = jnp.arange(2 * BLOCK)[None, :] - BLOCK
    rel = q_pos - k_pos
    in_window = (rel >= 0) & (rel < WINDOW)
    blk = jnp.arange(nb)[:, None, None]
    mask = in_window[None] & ((blk > 0) | (k_pos[None] >= 0))
    s = jnp.where(mask[None, :, None, None], s, NEG_INF)
    sk = sinks.astype(jnp.float32).reshape(1, 1, KV_HEADS, GROUP, 1, 1)
    m = jnp.maximum(jnp.max(s, axis=-1, keepdims=True), sk)
    p = jnp.exp(s - m)
    denom = jnp.sum(p, axis=-1, keepdims=True) + jnp.exp(sk - m)
    o = jnp.einsum('bnhgqk,bnkhd->bnqhgd', (p / denom).astype(vb.dtype), vb)
    return o.reshape(B, T, ATTN_HEADS * HEAD_DIM) @ w_o


def setup_inputs(seed: int = 0) -> dict:
    key = jax.random.key(seed)
    ks = jax.random.split(key, 27)
    nrm = lambda k, shape, scale: jax.random.normal(k, shape, jnp.float32) * scale
    D, NA, NB = D_MODEL, N_A_LAYERS, N_B_LAYERS
    return {
        'x': nrm(ks[0], (BATCH, SEQ, D), 1.0),
        'norm_g': 1.0 + nrm(ks[1], (DEPTH, 4, D), 0.05),
        'mu': jax.random.uniform(ks[2], (NA, 6, D), jnp.float32),
        'w_rkv': nrm(ks[3], (NA, 3, D, D), D ** -0.5),
        'w_decay0': jax.random.uniform(ks[4], (NA, D), jnp.float32, -6.5, -1.5),
        'w_decay1': nrm(ks[5], (NA, D, DECAY_LORA), D ** -0.5),
        'w_decay2': nrm(ks[6], (NA, DECAY_LORA, D), 0.1 * DECAY_LORA ** -0.5),
        'a0': nrm(ks[7], (NA, D), 0.1),
        'a1': nrm(ks[8], (NA, D, AAA_LORA), D ** -0.5),
        'a2': nrm(ks[9], (NA, AAA_LORA, D), 0.5 * AAA_LORA ** -0.5),
        'g1': nrm(ks[10], (NA, D, GATE_LORA), D ** -0.5),
        'g2': nrm(ks[11], (NA, GATE_LORA, D), GATE_LORA ** -0.5),
        'k_k': 0.85 + nrm(ks[12], (NA, D), 0.05),
        'k_a': 1.0 + nrm(ks[13], (NA, D), 0.05),
        'r_k': nrm(ks[14], (NA, RWKV_HEADS, HEAD_DIM), 0.1),
        'gn_g': 1.0 + nrm(ks[15], (NA, D), 0.05),
        'gn_b': nrm(ks[16], (NA, D), 0.01),
        'w_o_rwkv': nrm(ks[17], (NA, D, D), D ** -0.5),
        'kv_g': 1.0 + nrm(ks[18], (D,), 0.05),
        'w_kv': nrm(ks[19], (D, 2 * KV_HEADS * HEAD_DIM), D ** -0.5),
        'w_q': nrm(ks[20], (NB, D, ATTN_HEADS * HEAD_DIM), D ** -0.5),
        'sinks': nrm(ks[21], (NB, ATTN_HEADS), 0.5),
        'w_o_attn': nrm(ks[22], (NB, ATTN_HEADS * HEAD_DIM, D), (ATTN_HEADS * HEAD_DIM) ** -0.5),
        'w_up': nrm(ks[23], (DEPTH, D, 2 * D_FF), D ** -0.5),
        'conv_w': nrm(ks[24], (DEPTH, CONV_WIDTH, 2 * D_FF), CONV_WIDTH ** -0.5),
        'conv_b': nrm(ks[25], (DEPTH, 2 * D_FF), 0.01),
        'w_down': nrm(ks[26], (DEPTH, D_FF, D), D_FF ** -0.5),
    }


def reference(x, norm_g, mu, w_rkv, w_decay0, w_decay1, w_decay2, a0, a1, a2, g1, g2,
              k_k, k_a, r_k, gn_g, gn_b, w_o_rwkv, kv_g, w_kv, w_q, sinks, w_o_attn,
              w_up, conv_w, conv_b, w_down):
    k_sh = v_sh = None
    for l in range(DEPTH):
        h = rms_norm(x, norm_g[l, 0])
        if l < N_A_LAYERS:
            i = l
            h = rwkv7_time_mix(h, mu[i], w_rkv[i], w_decay0[i], w_decay1[i], w_decay2[i],
                               a0[i], a1[i], a2[i], g1[i], g2[i], k_k[i], k_a[i], r_k[i],
                               gn_g[i], gn_b[i], w_o_rwkv[i])
        else:
            j = l - N_A_LAYERS
            h = swa_sink_attention(h, k_sh, v_sh, w_q[j], sinks[j], w_o_attn[j])
        x = x + rms_norm(h, norm_g[l, 1])
        h = conv_ffn(rms_norm(x, norm_g[l, 2]), w_up[l], conv_w[l], conv_b[l], w_down[l])
        x = x + rms_norm(h, norm_g[l, 3])
        if l == N_A_LAYERS - 1:
            k_sh, v_sh = shared_kv(x, kv_g, w_kv)
    return x
```

```python
import functools
import math

import jax
import jax.numpy as jnp
from jax import lax
from jax.experimental import pallas as pl
from jax.experimental.pallas import tpu as pltpu

F32 = jnp.float32
BF16 = jnp.bfloat16

HEAD_DIM = 64
KV_HEADS = 4
GROUP = 4
WINDOW = 128
RMS_EPS = 1e-6
GN_EPS = 64e-5
NEG_INF = -1e30

LANES = 128
SUBLANES = 8
BF16_ROWS = 16
PAIR = LANES
CHUNK = 64

VMEM_LIMIT = 56 * 1024 * 1024


def _dot(a, b):
    return jnp.dot(a, b, preferred_element_type=F32)


def _dot_nt(a, b):
    return lax.dot_general(a, b, (((1,), (1,)), ((), ())), preferred_element_type=F32)


def _dot_tn(a, b):
    return lax.dot_general(a, b, (((0,), (0,)), ((), ())), preferred_element_type=F32)


def _rms(x, g):
    return x * lax.rsqrt(jnp.mean(x * x, axis=-1, keepdims=True) + RMS_EPS) * g


def _split_bf16(x):
    hi = x.astype(BF16)
    lo = (x - hi.astype(F32)).astype(BF16)
    return hi, lo


def _div_pow2(x, n):
    assert n & (n - 1) == 0
    return x >> (n.bit_length() - 1)


def _block_ones(n, blk, dtype, value=1.0):
    r = _div_pow2(lax.broadcasted_iota(jnp.int32, (n, n), 0), blk)
    c = _div_pow2(lax.broadcasted_iota(jnp.int32, (n, n), 1), blk)
    return jnp.where(r == c, value, 0.0).astype(dtype)


def _const_spec(shape):
    return pl.BlockSpec(shape, lambda *_: (0,) * len(shape))


def _rwkv_pre_kernel(x_ref, xh_ref, g_ref, mu_ref, wr_ref, wk_ref, wv_ref,
                     w0_ref, w1_ref, w2_ref, a0_ref, a1_ref, a2_ref,
                     g1_ref, g2_ref, kk_ref, ka_ref,
                     r_out, lw_out, k_out, v_out, kk_out, bb_out, g_out,
                     *, tiles_per_seq):
    i = pl.program_id(0)
    g = g_ref[...]
    h = _rms(x_ref[...], g)
    hp = _rms(xh_ref[...], g)
    hp = jnp.where(i % tiles_per_seq == 0, 0.0, hp)
    hcat = jnp.concatenate([hp, h], axis=0)
    prev = pltpu.roll(hcat, 1, 0)[SUBLANES:]
    xx = prev - h

    def mix(j):
        return (h + xx * mu_ref[j:j + 1, :]).astype(BF16)

    r = _dot(mix(0), wr_ref[...])
    z = w0_ref[...] + _dot(jnp.tanh(_dot(mix(1), w1_ref[...])).astype(BF16), w2_ref[...])
    k = _dot(mix(2), wk_ref[...])
    v = _dot(mix(3), wv_ref[...])
    a = jax.nn.sigmoid(a0_ref[...] + _dot(_dot(mix(4), a1_ref[...]).astype(BF16), a2_ref[...]))
    gate = _dot(jax.nn.sigmoid(_dot(mix(5), g1_ref[...])).astype(BF16), g2_ref[...])

    lw = -jax.nn.sigmoid(z) * math.exp(-0.5)

    kkr = k * kk_ref[...]
    sq = kkr * kkr
    d = sq.shape[-1]
    ones4 = _block_ones(2 * PAIR, HEAD_DIM, BF16)
    parts = []
    for p in range(d // (2 * PAIR)):
        hi, lo = _split_bf16(sq[:, 2 * PAIR * p:2 * PAIR * (p + 1)])
        parts.append(_dot(hi, ones4) + _dot(lo, ones4))
    ss = jnp.concatenate(parts, axis=1)
    kk = kkr * lax.rsqrt(jnp.maximum(ss, 1e-24))

    r_out[...] = r
    lw_out[...] = lw
    k_out[...] = k * (1.0 + (a - 1.0) * ka_ref[...])
    v_out[...] = v
    kk_out[...] = kk
    bb_out[...] = kk * a
    g_out[...] = gate


def _rwkv_pre(x2, norm_g, mu, wr, wk, wv, w0, w1, w2, a0, a1, a2, g1, g2, k_k, k_a, seq):
    m, d = x2.shape
    tm = 256
    row = pl.BlockSpec((tm, d), lambda i: (i, 0))
    halo = pl.BlockSpec((SUBLANES, d), lambda i: (jnp.maximum(i * (tm // SUBLANES) - 1, 0), 0))
    ins = [x2, x2, norm_g, mu, wr, wk, wv, w0, w1, w2, a0, a1, a2, g1, g2, k_k, k_a]
    in_specs = [row, halo] + [_const_spec(t.shape) for t in ins[2:]]
    out = jax.ShapeDtypeStruct((m, d), F32)
    return pl.pallas_call(
        functools.partial(_rwkv_pre_kernel, tiles_per_seq=seq // tm),
        grid=(m // tm,),
        in_specs=in_specs,
        out_specs=[row] * 7,
        out_shape=[out] * 7,
        compiler_params=pltpu.CompilerParams(
            dimension_semantics=("arbitrary",), vmem_limit_bytes=VMEM_LIMIT),
        name="rwkv_pre",
    )(*ins)


def _scan_kernel(r_ref, lw_ref, k_ref, v_ref, kk_ref, bb_ref, g_ref,
                 rk_ref, gng_ref, gnb_ref, out_ref, s_ref, *, n_chunks):
    L = CHUNK
    n_pairs = r_ref.shape[1] // PAIR

    @pl.when(pl.program_id(1) == 0)
    def _():
        s_ref[...] = jnp.zeros_like(s_ref)

    lane = lax.broadcasted_iota(jnp.int32, (1, PAIR), 1)
    head0 = lane < HEAD_DIM

    def stack2(x):
        return jnp.concatenate([jnp.where(head0, x, 0.0), jnp.where(head0, 0.0, x)], axis=0)

    rr = lax.broadcasted_iota(jnp.int32, (2 * L, 2 * L), 0)
    cc = lax.broadcasted_iota(jnp.int32, (2 * L, 2 * L), 1)
    same_head = _div_pow2(rr, L) == _div_pow2(cc, L)
    eye = rr == cc
    level_masks = []
    s = 4
    while s <= L:
        level_masks.append((_div_pow2(rr, s) == _div_pow2(cc, s))
                           & (_div_pow2(rr, s // 2) != _div_pow2(cc, s // 2)))
        s *= 2
    base_mask = _div_pow2(rr, 2) == _div_pow2(cc, 2)

    tt = lax.broadcasted_iota(jnp.int32, (L, 2 * L), 0)
    ss_ = lax.broadcasted_iota(jnp.int32, (L, 2 * L), 1) & (L - 1)
    strict = tt > ss_
    incl = tt >= ss_
    tri = (lax.broadcasted_iota(jnp.int32, (L, L), 0)
           >= lax.broadcasted_iota(jnp.int32, (L, L), 1)).astype(BF16)

    ones2 = _block_ones(PAIR, HEAD_DIM, BF16)
    mean2 = _block_ones(PAIR, HEAD_DIM, BF16, 1.0 / HEAD_DIM)

    def chunk_body(ci, carry):
        t0 = pl.multiple_of(ci * L, L)
        for p in range(n_pairs):
            rows = pl.ds(t0, L)
            cols = slice(PAIR * p, PAIR * (p + 1))
            r = r_ref[rows, cols]
            lw = lw_ref[rows, cols]
            k = k_ref[rows, cols]
            v = v_ref[rows, cols]
            kk = kk_ref[rows, cols]
            bb = bb_ref[rows, cols]

            lw_hi, lw_lo = _split_bf16(lw)
            cum = _dot(tri, lw_hi) + _dot(tri, lw_lo)
            total = cum[L - 1:L, :]
            e_pos = jnp.exp(cum)
            e_neg = jnp.exp(-cum)
            e_end = jnp.exp(total - cum)
            rt = r * e_pos
            kt = k * e_neg
            at = -kk * jnp.exp(cum - lw)
            bt = bb * e_neg
            vst = stack2(v).astype(BF16)

            lhs = jnp.concatenate([at, rt], axis=0).astype(BF16)
            rhs = jnp.concatenate([stack2(bt), stack2(kt)], axis=0).astype(BF16)
            gmat = _dot_nt(lhs, rhs)
            h_ab = jnp.where(strict, gmat[:L, :2 * L], 0.0)
            h_ak = jnp.where(strict, gmat[:L, 2 * L:], 0.0)
            h_rb = jnp.where(incl, gmat[L:, :2 * L], 0.0)
            h_rk = jnp.where(incl, gmat[L:, 2 * L:], 0.0)

            nbd = stack2(h_ab)
            tinv = jnp.where(eye, 1.0, jnp.where(base_mask, nbd, 0.0))
            for lm in level_masks:
                tb = tinv.astype(BF16)
                x = _dot(tb, jnp.where(lm, nbd, 0.0).astype(BF16))
                tinv = tinv + _dot(x.astype(BF16), tb)
            th = (tinv[:L] + tinv[L:]).astype(BF16)

            zy = _dot(jnp.concatenate([h_ak, h_rk], axis=0).astype(BF16), vst)
            z = zy[:L]
            y_intra = zy[L:]
            aw = _dot(th, jnp.concatenate([stack2(at), stack2(z)], axis=1).astype(BF16))
            a_eff = aw[:, :PAIR]
            w_eff = aw[:, PAIR:]

            state = s_ref[p]
            uy = _dot_nt(jnp.concatenate([a_eff, rt], axis=0).astype(BF16), state.astype(BF16))
            u = uy[:L] + w_eff
            y = uy[L:] + y_intra + _dot(h_rb.astype(BF16), stack2(u).astype(BF16))

            upd = _dot_tn(jnp.concatenate([u, v], axis=0).astype(BF16),
                          jnp.concatenate([bb * e_end, k * e_end], axis=0).astype(BF16))
            s_ref[p] = state * jnp.exp(total) + jnp.where(same_head, upd, 0.0)

            y_hi, y_lo = _split_bf16(y)
            yc = y - (_dot(y_hi, mean2) + _dot(y_lo, mean2))
            var = _dot((yc * yc).astype(BF16), mean2)
            yn = yc * lax.rsqrt(var + GN_EPS) * gng_ref[:, cols] + gnb_ref[:, cols]
            bonus = _dot((r * k * rk_ref[:, cols]).astype(BF16), ones2) * v
            out_ref[rows, cols] = ((yn + bonus) * g_ref[rows, cols]).astype(out_ref.dtype)
        return carry

    lax.fori_loop(0, n_chunks, chunk_body, 0)


def _rwkv_scan(r, lw, k, v, kk, bb, g, r_k, gn_g, gn_b, batch, seq):
    m, d = r.shape
    tt = 256
    steps = seq // tt
    row = pl.BlockSpec((tt, d), lambda b, c: (b * steps + c, 0))
    vec = pl.BlockSpec((1, d), lambda b, c: (0, 0))
    return pl.pallas_call(
        functools.partial(_scan_kernel, n_chunks=tt // CHUNK),
        grid=(batch, steps),
        in_specs=[row] * 7 + [vec] * 3,
        out_specs=row,
        out_shape=jax.ShapeDtypeStruct((m, d), BF16),
        scratch_shapes=[pltpu.VMEM((d // PAIR, PAIR, PAIR), F32)],
        compiler_params=pltpu.CompilerParams(
            dimension_semantics=("arbitrary", "arbitrary"), vmem_limit_bytes=VMEM_LIMIT),
        name="rwkv_scan",
    )(r, lw, k, v, kk, bb, g, r_k, gn_g, gn_b)


def _proj_norm_res_kernel(y_ref, w_ref, x_ref, g_ref, out_ref):
    h = _dot(y_ref[...], w_ref[...])
    out_ref[...] = x_ref[...] + _rms(h, g_ref[...])


def _proj_norm_res(y, w, x2, g):
    m, d = x2.shape
    tm = 512
    row_y = pl.BlockSpec((tm, y.shape[1]), lambda i: (i, 0))
    row_x = pl.BlockSpec((tm, d), lambda i: (i, 0))
    return pl.pallas_call(
        _proj_norm_res_kernel,
        grid=(m // tm,),
        in_specs=[row_y, _const_spec(w.shape), row_x, _const_spec(g.shape)],
        out_specs=row_x,
        out_shape=jax.ShapeDtypeStruct((m, d), F32),
        compiler_params=pltpu.CompilerParams(
            dimension_semantics=("arbitrary",), vmem_limit_bytes=VMEM_LIMIT),
        name="proj_norm_res",
    )(y, w, x2, g)


def _ffn_kernel(x_ref, xh_ref, gin_ref, wup_ref, cw_ref, cb_ref, wdn_ref, gout_ref,
                out_ref, acc_ref, *, tiles_per_seq, d_ff, f_chunk):
    i = pl.program_id(0)
    x = x_ref[...]
    gin = gin_ref[...]
    xn = _rms(x, gin)
    xp = _rms(xh_ref[...], gin)
    xp = jnp.where(i % tiles_per_seq == 0, 0.0, xp)
    xcat = jnp.concatenate([xp, xn], axis=0).astype(BF16)
    halo = xh_ref.shape[0]
    c_gelu = math.sqrt(2.0 / math.pi)

    def conv(hfull, col):
        w = cw_ref[:, col]
        h1 = pltpu.roll(hfull, 1, 0)[halo:]
        h2 = pltpu.roll(hfull, 2, 0)[halo:]
        return hfull[halo:] * w[2:3] + h1 * w[1:2] + h2 * w[0:1] + cb_ref[:, col]

    for j in range(d_ff // f_chunk):
        ca = slice(j * f_chunk, (j + 1) * f_chunk)
        cu = slice(d_ff + j * f_chunk, d_ff + (j + 1) * f_chunk)
        a = conv(_dot(xcat, wup_ref[:, ca]), ca)
        u = conv(_dot(xcat, wup_ref[:, cu]), cu)
        gelu = a * (0.5 * (1.0 + jnp.tanh(c_gelu * (a + 0.044715 * (a * a * a)))))
        part = _dot((gelu * u).astype(BF16), wdn_ref[ca, :])
        if j == 0:
            acc_ref[...] = part
        else:
            acc_ref[...] += part
    out_ref[...] = x + _rms(acc_ref[...], gout_ref[...])


def _conv_ffn(x2, g_in, w_up, conv_w, conv_b, w_down, g_out, seq):
    m, d = x2.shape
    d_ff = w_down.shape[0]
    tm = 512
    f_chunk = 256
    row = pl.BlockSpec((tm, d), lambda i: (i, 0))
    halo = pl.BlockSpec((BF16_ROWS, d), lambda i: (jnp.maximum(i * (tm // BF16_ROWS) - 1, 0), 0))
    single = pl.Buffered(1)
    in_specs = [
        row, halo, _const_spec(g_in.shape),
        pl.BlockSpec(w_up.shape, lambda i: (0, 0), pipeline_mode=single),
        _const_spec(conv_w.shape), _const_spec(conv_b.shape),
        pl.BlockSpec(w_down.shape, lambda i: (0, 0), pipeline_mode=single),
        _const_spec(g_out.shape),
    ]
    return pl.pallas_call(
        functools.partial(_ffn_kernel, tiles_per_seq=seq // tm, d_ff=d_ff, f_chunk=f_chunk),
        grid=(m // tm,),
        in_specs=in_specs,
        out_specs=row,
        out_shape=jax.ShapeDtypeStruct((m, d), F32),
        scratch_shapes=[pltpu.VMEM((tm, d), F32)],
        compiler_params=pltpu.CompilerParams(
            dimension_semantics=("arbitrary",), vmem_limit_bytes=VMEM_LIMIT),
        name="conv_ffn",
    )(x2, x2, g_in, w_up, conv_w, conv_b, w_down, g_out)


def _attn_pre_kernel(x_ref, gq_ref, gkv_ref, wq_ref, wkv_ref, q_out, kv_out):
    x = x_ref[...]
    xn = x * lax.rsqrt(jnp.mean(x * x, axis=-1, keepdims=True) + RMS_EPS)
    q = _dot((xn * gq_ref[...]).astype(BF16), wq_ref[...])
    q_out[...] = (q * (HEAD_DIM ** -0.5)).astype(q_out.dtype)
    kv_out[...] = _dot((xn * gkv_ref[...]).astype(BF16), wkv_ref[...]).astype(kv_out.dtype)


def _attn_pre(x2, g_q, g_kv, w_q, w_kv):
    m, d = x2.shape
    tm = 512
    row = pl.BlockSpec((tm, d), lambda i: (i, 0))
    return pl.pallas_call(
        _attn_pre_kernel,
        grid=(m // tm,),
        in_specs=[row, _const_spec(g_q.shape), _const_spec(g_kv.shape),
                  _const_spec(w_q.shape), _const_spec(w_kv.shape)],
        out_specs=[pl.BlockSpec((tm, w_q.shape[1]), lambda i: (i, 0)),
                   pl.BlockSpec((tm, w_kv.shape[1]), lambda i: (i, 0))],
        out_shape=[jax.ShapeDtypeStruct((m, w_q.shape[1]), BF16),
                   jax.ShapeDtypeStruct((m, w_kv.shape[1]), BF16)],
        compiler_params=pltpu.CompilerParams(
            dimension_semantics=("arbitrary",), vmem_limit_bytes=VMEM_LIMIT),
        name="attn_pre",
    )(x2, g_q, g_kv, w_q, w_kv)


def _attn_kernel(sinks_ref, q_ref, kvp_ref, kvo_ref, out_ref, *, blocks_per_seq):
    blk = q_ref.shape[0]
    n_pairs = q_ref.shape[1] // PAIR
    kv_width = KV_HEADS * HEAD_DIM
    first = (pl.program_id(0) % blocks_per_seq) == 0

    lane = lax.broadcasted_iota(jnp.int32, (1, PAIR), 1)
    head0 = lane < HEAD_DIM
    row = lax.broadcasted_iota(jnp.int32, (2 * blk, 2 * blk), 0)
    col = lax.broadcasted_iota(jnp.int32, (2 * blk, 2 * blk), 1)
    rel = (row & (blk - 1)) - (col - blk)
    valid = (rel >= 0) & (rel < WINDOW) & (jnp.logical_not(first) | (col >= blk))
    top = lax.broadcasted_iota(jnp.int32, (2 * blk, 1), 0) < blk

    kv = jnp.concatenate([kvp_ref[...], kvo_ref[...]], axis=0)

    def dup(x, odd):
        rolled = pltpu.roll(x, HEAD_DIM, 1)
        return jnp.where(head0, rolled, x) if odd else jnp.where(head0, x, rolled)

    for p in range(n_pairs):
        h = (2 * p) // GROUP
        kcol = (h // 2) * PAIR
        kd = dup(kv[:, kcol:kcol + PAIR], h % 2 == 1)
        vd = dup(kv[:, kv_width + kcol:kv_width + kcol + PAIR], h % 2 == 1)
        q = q_ref[:, PAIR * p:PAIR * (p + 1)]
        zero = jnp.zeros_like(q)
        qs = jnp.concatenate([jnp.where(head0, q, zero), jnp.where(head0, zero, q)], axis=0)
        s = jnp.where(valid, _dot_nt(qs, kd), NEG_INF)
        sink = jnp.where(top, sinks_ref[2 * p], sinks_ref[2 * p + 1])
        mx = jnp.maximum(jnp.max(s, axis=-1, keepdims=True), sink)
        pr = jnp.exp(s - mx)
        denom = jnp.sum(pr, axis=-1, keepdims=True) + jnp.exp(sink - mx)
        o2 = _dot(pr.astype(BF16), vd) / denom
        out_ref[:, PAIR * p:PAIR * (p + 1)] = jnp.where(head0, o2[:blk], o2[blk:]).astype(out_ref.dtype)


def _attention(q, kv, sinks, batch, seq):
    m, d = q.shape
    blk = WINDOW
    nb = seq // blk
    return pl.pallas_call(
        functools.partial(_attn_kernel, blocks_per_seq=nb),
        grid=(m // blk,),
        in_specs=[
            pl.BlockSpec(memory_space=pltpu.SMEM),
            pl.BlockSpec((blk, d), lambda i: (i, 0)),
            pl.BlockSpec((blk, kv.shape[1]), lambda i: (jnp.maximum(i - 1, 0), 0)),
            pl.BlockSpec((blk, kv.shape[1]), lambda i: (i, 0)),
        ],
        out_specs=pl.BlockSpec((blk, d), lambda i: (i, 0)),
        out_shape=jax.ShapeDtypeStruct((m, d), BF16),
        compiler_params=pltpu.CompilerParams(
            dimension_semantics=("arbitrary",), vmem_limit_bytes=VMEM_LIMIT),
        name="swa_attention",
    )(sinks, q, kv, kv)


def kernel(x, norm_g, mu, w_rkv, w_decay0, w_decay1, w_decay2, a0, a1, a2, g1, g2, k_k, k_a, r_k, gn_g, gn_b, w_o_rwkv, kv_g, w_kv, w_q, sinks, w_o_attn, w_up, conv_w, conv_b, w_down):
    batch, seq, d = x.shape
    n_rwkv = mu.shape[0]
    depth = norm_g.shape[0]
    bf = lambda t: t.astype(BF16)
    vec = lambda t: t.reshape(1, -1)
    x2 = x.reshape(batch * seq, d)
    kv = None
    for l in range(depth):
        if l < n_rwkv:
            i = l
            parts = _rwkv_pre(
                x2, vec(norm_g[l, 0]), mu[i], bf(w_rkv[i, 0]), bf(w_rkv[i, 1]), bf(w_rkv[i, 2]),
                vec(w_decay0[i]), bf(w_decay1[i]), bf(w_decay2[i]),
                vec(a0[i]), bf(a1[i]), bf(a2[i]), bf(g1[i]), bf(g2[i]),
                vec(k_k[i]), vec(k_a[i]), seq)
            y = _rwkv_scan(*parts, vec(r_k[i]), vec(gn_g[i]), vec(gn_b[i]), batch, seq)
            x2 = _proj_norm_res(y, bf(w_o_rwkv[i]), x2, vec(norm_g[l, 1]))
        else:
            j = l - n_rwkv
            if j == 0:
                q, kv = _attn_pre(x2, vec(norm_g[l, 0]), vec(kv_g), bf(w_q[j]), bf(w_kv))
            else:
                q, _ = _attn_pre(x2, vec(norm_g[l, 0]), vec(kv_g), bf(w_q[j]), bf(w_kv))
            o = _attention(q, kv, sinks[j], batch, seq)
            x2 = _proj_norm_res(o, bf(w_o_attn[j]), x2, vec(norm_g[l, 1]))
        x2 = _conv_ffn(x2, vec(norm_g[l, 2]), bf(w_up[l]), conv_w[l], vec(conv_b[l]),
                       bf(w_down[l]), vec(norm_g[l, 3]), seq)
    return x2.reshape(batch, seq, d)
```

```python
import functools
import math

import jax
import jax.numpy as jnp
from jax import lax
from jax.experimental import pallas as pl
from jax.experimental.pallas import tpu as pltpu

F32 = jnp.float32
BF16 = jnp.bfloat16

HEAD_DIM = 64
KV_HEADS = 4
GROUP = 4
WINDOW = 128
RMS_EPS = 1e-6
GN_EPS = 64e-5
NEG_INF = -1e30

LANES = 128
SUBLANES = 8
BF16_ROWS = 16
PAIR = LANES
CHUNK = 64

VMEM_LIMIT = 56 * 1024 * 1024


def _dot(a, b):
    return jnp.dot(a, b, preferred_element_type=F32)


def _dot_nt(a, b):
    return lax.dot_general(a, b, (((1,), (1,)), ((), ())), preferred_element_type=F32)


def _dot_tn(a, b):
    return lax.dot_general(a, b, (((0,), (0,)), ((), ())), preferred_element_type=F32)


def _rms(x, g):
    return x * lax.rsqrt(jnp.mean(x * x, axis=-1, keepdims=True) + RMS_EPS) * g


def _split_bf16(x):
    hi = x.astype(BF16)
    lo = (x - hi.astype(F32)).astype(BF16)
    return hi, lo


def _div_pow2(x, n):
    assert n & (n - 1) == 0
    return x >> (n.bit_length() - 1)


def _block_ones(n, blk, dtype, value=1.0):
    r = _div_pow2(lax.broadcasted_iota(jnp.int32, (n, n), 0), blk)
    c = _div_pow2(lax.broadcasted_iota(jnp.int32, (n, n), 1), blk)
    return jnp.where(r == c, value, 0.0).astype(dtype)


def _const_spec(shape):
    return pl.BlockSpec(shape, lambda *_: (0,) * len(shape))


def _rwkv_pre_kernel(x_ref, xh_ref, g_ref, mu_ref, wr_ref, wk_ref, wv_ref,
                     w0_ref, w1_ref, w2_ref, a0_ref, a1_ref, a2_ref,
                     g1_ref, g2_ref, kk_ref, ka_ref,
                     r_out, lw_out, k_out, v_out, kk_out, bb_out, g_out,
                     *, tiles_per_seq):
    i = pl.program_id(0)
    g = g_ref[...]
    h = _rms(x_ref[...], g)
    hp = _rms(xh_ref[...], g)
    hp = jnp.where(i % tiles_per_seq == 0, 0.0, hp)
    hcat = jnp.concatenate([hp, h], axis=0)
    prev = pltpu.roll(hcat, 1, 0)[SUBLANES:]
    xx = prev - h

    def mix(j):
        return (h + xx * mu_ref[j:j + 1, :]).astype(BF16)

    r = _dot(mix(0), wr_ref[...])
    z = w0_ref[...] + _dot(jnp.tanh(_dot(mix(1), w1_ref[...])).astype(BF16), w2_ref[...])
    k = _dot(mix(2), wk_ref[...])
    v = _dot(mix(3), wv_ref[...])
    a = jax.nn.sigmoid(a0_ref[...] + _dot(_dot(mix(4), a1_ref[...]).astype(BF16), a2_ref[...]))
    gate = _dot(jax.nn.sigmoid(_dot(mix(5), g1_ref[...])).astype(BF16), g2_ref[...])

    lw = -jax.nn.sigmoid(z) * math.exp(-0.5)

    kkr = k * kk_ref[...]
    sq = kkr * kkr
    d = sq.shape[-1]
    ones4 = _block_ones(2 * PAIR, HEAD_DIM, BF16)
    parts = []
    for p in range(d // (2 * PAIR)):
        hi, lo = _split_bf16(sq[:, 2 * PAIR * p:2 * PAIR * (p + 1)])
        parts.append(_dot(hi, ones4) + _dot(lo, ones4))
    ss = jnp.concatenate(parts, axis=1)
    kk = kkr * lax.rsqrt(jnp.maximum(ss, 1e-24))

    r_out[...] = r
    lw_out[...] = lw
    k_out[...] = k * (1.0 + (a - 1.0) * ka_ref[...])
    v_out[...] = v
    kk_out[...] = kk
    bb_out[...] = kk * a
    g_out[...] = gate


def _rwkv_pre(x2, norm_g, mu, wr, wk, wv, w0, w1, w2, a0, a1, a2, g1, g2, k_k, k_a, seq):
    m, d = x2.shape
    tm = 256
    row = pl.BlockSpec((tm, d), lambda i: (i, 0))
    halo = pl.BlockSpec((SUBLANES, d), lambda i: (jnp.maximum(i * (tm // SUBLANES) - 1, 0), 0))
    ins = [x2, x2, norm_g, mu, wr, wk, wv, w0, w1, w2, a0, a1, a2, g1, g2, k_k, k_a]
    in_specs = [row, halo] + [_const_spec(t.shape) for t in ins[2:]]
    out = jax.ShapeDtypeStruct((m, d), F32)
    return pl.pallas_call(
        functools.partial(_rwkv_pre_kernel, tiles_per_seq=seq // tm),
        grid=(m // tm,),
        in_specs=in_specs,
        out_specs=[row] * 7,
        out_shape=[out] * 7,
        compiler_params=pltpu.CompilerParams(
            dimension_semantics=("arbitrary",), vmem_limit_bytes=VMEM_LIMIT),
        name="rwkv_pre",
    )(*ins)


def _scan_kernel(r_ref, lw_ref, k_ref, v_ref, kk_ref, bb_ref, g_ref,
                 rk_ref, gng_ref, gnb_ref, out_ref, s_ref, *, n_chunks):
    L = CHUNK
    n_pairs = r_ref.shape[1] // PAIR

    @pl.when(pl.program_id(1) == 0)
    def _():
        s_ref[...] = jnp.zeros_like(s_ref)

    lane = lax.broadcasted_iota(jnp.int32, (1, PAIR), 1)
    head0 = lane < HEAD_DIM

    def stack2(x):
        return jnp.concatenate([jnp.where(head0, x, 0.0), jnp.where(head0, 0.0, x)], axis=0)

    rr = lax.broadcasted_iota(jnp.int32, (2 * L, 2 * L), 0)
    cc = lax.broadcasted_iota(jnp.int32, (2 * L, 2 * L), 1)
    same_head = _div_pow2(rr, L) == _div_pow2(cc, L)
    eye = rr == cc
    level_masks = []
    s = 4
    while s <= L:
        level_masks.append((_div_pow2(rr, s) == _div_pow2(cc, s))
                           & (_div_pow2(rr, s // 2) != _div_pow2(cc, s // 2)))
        s *= 2
    base_mask = _div_pow2(rr, 2) == _div_pow2(cc, 2)

    tt = lax.broadcasted_iota(jnp.int32, (L, 2 * L), 0)
    ss_ = lax.broadcasted_iota(jnp.int32, (L, 2 * L), 1) & (L - 1)
    strict = tt > ss_
    incl = tt >= ss_
    tri = (lax.broadcasted_iota(jnp.int32, (L, L), 0)
           >= lax.broadcasted_iota(jnp.int32, (L, L), 1)).astype(BF16)

    ones2 = _block_ones(PAIR, HEAD_DIM, BF16)
    mean2 = _block_ones(PAIR, HEAD_DIM, BF16, 1.0 / HEAD_DIM)

    pairs = range(n_pairs)
    col = [slice(PAIR * p, PAIR * (p + 1)) for p in pairs]

    def chunk_body(ci, carry):
        rows = pl.ds(pl.multiple_of(ci * L, L), L)
        ld = lambda ref: [ref[rows, col[p]] for p in pairs]

        lw = ld(lw_ref)
        split = [_split_bf16(x) for x in lw]
        cum = [_dot(tri, hi) + _dot(tri, lo) for hi, lo in split]
        total = [c[L - 1:L, :] for c in cum]
        e_neg = [jnp.exp(-c) for c in cum]
        rt = [x * jnp.exp(c) for x, c in zip(ld(r_ref), cum)]
        kt = [x * e for x, e in zip(ld(k_ref), e_neg)]
        at = [-x * jnp.exp(c - w) for x, c, w in zip(ld(kk_ref), cum, lw)]
        bt = [x * e for x, e in zip(ld(bb_ref), e_neg)]

        gmat = [_dot_nt(jnp.concatenate([at[p], rt[p]], axis=0).astype(BF16),
                        jnp.concatenate([stack2(bt[p]), stack2(kt[p])], axis=0).astype(BF16))
                for p in pairs]
        h_ak = [jnp.where(strict, g_[:L, 2 * L:], 0.0) for g_ in gmat]
        h_rb = [jnp.where(incl, g_[L:, :2 * L], 0.0).astype(BF16) for g_ in gmat]
        h_rk = [jnp.where(incl, g_[L:, 2 * L:], 0.0) for g_ in gmat]

        nbd = [stack2(jnp.where(strict, g_[:L, :2 * L], 0.0)) for g_ in gmat]
        tinv = [jnp.where(eye, 1.0, jnp.where(base_mask, n_, 0.0)) for n_ in nbd]
        for lm in level_masks:
            tb = [t_.astype(BF16) for t_ in tinv]
            x = [_dot(tb[p], jnp.where(lm, nbd[p], 0.0).astype(BF16)) for p in pairs]
            tinv = [tinv[p] + _dot(x[p].astype(BF16), tb[p]) for p in pairs]
        th = [(t_[:L] + t_[L:]).astype(BF16) for t_ in tinv]

        v = ld(v_ref)
        zy = [_dot(jnp.concatenate([h_ak[p], h_rk[p]], axis=0).astype(BF16),
                   stack2(v[p]).astype(BF16)) for p in pairs]
        aw = [_dot(th[p], jnp.concatenate([stack2(at[p]), stack2(zy[p][:L])], axis=1).astype(BF16))
              for p in pairs]

        state = [s_ref[p] for p in pairs]
        uy = [_dot_nt(jnp.concatenate([aw[p][:, :PAIR], rt[p]], axis=0).astype(BF16),
                      state[p].astype(BF16)) for p in pairs]
        u = [uy[p][:L] + aw[p][:, PAIR:] for p in pairs]
        y = [uy[p][L:] + zy[p][L:] + _dot(h_rb[p], stack2(u[p]).astype(BF16)) for p in pairs]

        k = ld(k_ref)
        bb = ld(bb_ref)
        e_end = [jnp.exp(total[p] - cum[p]) for p in pairs]
        upd = [_dot_tn(jnp.concatenate([u[p], v[p]], axis=0).astype(BF16),
                       jnp.concatenate([bb[p] * e_end[p], k[p] * e_end[p]], axis=0).astype(BF16))
               for p in pairs]
        for p in pairs:
            s_ref[p] = state[p] * jnp.exp(total[p]) + jnp.where(same_head, upd[p], 0.0)

        ysplit = [_split_bf16(y_) for y_ in y]
        yc = [y[p] - (_dot(ysplit[p][0], mean2) + _dot(ysplit[p][1], mean2)) for p in pairs]
        var = [_dot((c * c).astype(BF16), mean2) for c in yc]
        r = ld(r_ref)
        bonus = [_dot((r[p] * k[p] * rk_ref[:, col[p]]).astype(BF16), ones2) * v[p] for p in pairs]
        for p in pairs:
            yn = yc[p] * lax.rsqrt(var[p] + GN_EPS) * gng_ref[:, col[p]] + gnb_ref[:, col[p]]
            out_ref[rows, col[p]] = ((yn + bonus[p]) * g_ref[rows, col[p]]).astype(out_ref.dtype)
        return carry

    lax.fori_loop(0, n_chunks, chunk_body, 0)


def _rwkv_scan(r, lw, k, v, kk, bb, g, r_k, gn_g, gn_b, batch, seq):
    m, d = r.shape
    tt = 256
    steps = seq // tt
    row = pl.BlockSpec((tt, d), lambda b, c: (b * steps + c, 0))
    vec = pl.BlockSpec((1, d), lambda b, c: (0, 0))
    return pl.pallas_call(
        functools.partial(_scan_kernel, n_chunks=tt // CHUNK),
        grid=(batch, steps),
        in_specs=[row] * 7 + [vec] * 3,
        out_specs=row,
        out_shape=jax.ShapeDtypeStruct((m, d), BF16),
        scratch_shapes=[pltpu.VMEM((d // PAIR, PAIR, PAIR), F32)],
        compiler_params=pltpu.CompilerParams(
            dimension_semantics=("arbitrary", "arbitrary"), vmem_limit_bytes=VMEM_LIMIT),
        name="rwkv_scan",
    )(r, lw, k, v, kk, bb, g, r_k, gn_g, gn_b)


def _proj_norm_res_kernel(y_ref, w_ref, x_ref, g_ref, out_ref):
    h = _dot(y_ref[...], w_ref[...])
    out_ref[...] = x_ref[...] + _rms(h, g_ref[...])


def _proj_norm_res(y, w, x2, g):
    m, d = x2.shape
    tm = 512
    row_y = pl.BlockSpec((tm, y.shape[1]), lambda i: (i, 0))
    row_x = pl.BlockSpec((tm, d), lambda i: (i, 0))
    return pl.pallas_call(
        _proj_norm_res_kernel,
        grid=(m // tm,),
        in_specs=[row_y, _const_spec(w.shape), row_x, _const_spec(g.shape)],
        out_specs=row_x,
        out_shape=jax.ShapeDtypeStruct((m, d), F32),
        compiler_params=pltpu.CompilerParams(
            dimension_semantics=("arbitrary",), vmem_limit_bytes=VMEM_LIMIT),
        name="proj_norm_res",
    )(y, w, x2, g)


def _ffn_kernel(x_ref, xh_ref, gin_ref, wup_ref, cw_ref, cb_ref, wdn_ref, gout_ref,
                out_ref, acc_ref, *, tiles_per_seq, d_ff, f_chunk):
    i = pl.program_id(0)
    x = x_ref[...]
    gin = gin_ref[...]
    xn = _rms(x, gin)
    xp = _rms(xh_ref[...], gin)
    xp = jnp.where(i % tiles_per_seq == 0, 0.0, xp)
    xcat = jnp.concatenate([xp, xn], axis=0).astype(BF16)
    halo = xh_ref.shape[0]
    c_gelu = math.sqrt(2.0 / math.pi)

    def conv(hfull, col):
        w = cw_ref[:, col]
        h1 = pltpu.roll(hfull, 1, 0)[halo:]
        h2 = pltpu.roll(hfull, 2, 0)[halo:]
        return hfull[halo:] * w[2:3] + h1 * w[1:2] + h2 * w[0:1] + cb_ref[:, col]

    for j in range(d_ff // f_chunk):
        ca = slice(j * f_chunk, (j + 1) * f_chunk)
        cu = slice(d_ff + j * f_chunk, d_ff + (j + 1) * f_chunk)
        a = conv(_dot(xcat, wup_ref[:, ca]), ca)
        u = conv(_dot(xcat, wup_ref[:, cu]), cu)
        gelu = a * (0.5 * (1.0 + jnp.tanh(c_gelu * (a + 0.044715 * (a * a * a)))))
        part = _dot((gelu * u).astype(BF16), wdn_ref[ca, :])
        if j == 0:
            acc_ref[...] = part
        else:
            acc_ref[...] += part
    out_ref[...] = x + _rms(acc_ref[...], gout_ref[...])


def _conv_ffn(x2, g_in, w_up, conv_w, conv_b, w_down, g_out, seq):
    m, d = x2.shape
    d_ff = w_down.shape[0]
    tm = 512
    f_chunk = 256
    row = pl.BlockSpec((tm, d), lambda i: (i, 0))
    halo = pl.BlockSpec((BF16_ROWS, d), lambda i: (jnp.maximum(i * (tm // BF16_ROWS) - 1, 0), 0))
    single = pl.Buffered(1)
    in_specs = [
        row, halo, _const_spec(g_in.shape),
        pl.BlockSpec(w_up.shape, lambda i: (0, 0), pipeline_mode=single),
        _const_spec(conv_w.shape), _const_spec(conv_b.shape),
        pl.BlockSpec(w_down.shape, lambda i: (0, 0), pipeline_mode=single),
        _const_spec(g_out.shape),
    ]
    return pl.pallas_call(
        functools.partial(_ffn_kernel, tiles_per_seq=seq // tm, d_ff=d_ff, f_chunk=f_chunk),
        grid=(m // tm,),
        in_specs=in_specs,
        out_specs=row,
        out_shape=jax.ShapeDtypeStruct((m, d), F32),
        scratch_shapes=[pltpu.VMEM((tm, d), F32)],
        compiler_params=pltpu.CompilerParams(
            dimension_semantics=("arbitrary",), vmem_limit_bytes=VMEM_LIMIT),
        name="conv_ffn",
    )(x2, x2, g_in, w_up, conv_w, conv_b, w_down, g_out)


def _attn_pre_kernel(x_ref, gq_ref, gkv_ref, wq_ref, wkv_ref, q_out, kv_out):
    x = x_ref[...]
    xn = x * lax.rsqrt(jnp.mean(x * x, axis=-1, keepdims=True) + RMS_EPS)
    q = _dot((xn * gq_ref[...]).astype(BF16), wq_ref[...])
    q_out[...] = (q * (HEAD_DIM ** -0.5)).astype(q_out.dtype)
    kv_out[...] = _dot((xn * gkv_ref[...]).astype(BF16), wkv_ref[...]).astype(kv_out.dtype)


def _attn_pre(x2, g_q, g_kv, w_q, w_kv):
    m, d = x2.shape
    tm = 512
    row = pl.BlockSpec((tm, d), lambda i: (i, 0))
    return pl.pallas_call(
        _attn_pre_kernel,
        grid=(m // tm,),
        in_specs=[row, _const_spec(g_q.shape), _const_spec(g_kv.shape),
                  _const_spec(w_q.shape), _const_spec(w_kv.shape)],
        out_specs=[pl.BlockSpec((tm, w_q.shape[1]), lambda i: (i, 0)),
                   pl.BlockSpec((tm, w_kv.shape[1]), lambda i: (i, 0))],
        out_shape=[jax.ShapeDtypeStruct((m, w_q.shape[1]), BF16),
                   jax.ShapeDtypeStruct((m, w_kv.shape[1]), BF16)],
        compiler_params=pltpu.CompilerParams(
            dimension_semantics=("arbitrary",), vmem_limit_bytes=VMEM_LIMIT),
        name="attn_pre",
    )(x2, g_q, g_kv, w_q, w_kv)


def _attn_kernel(sinks_ref, q_ref, kvp_ref, kvo_ref, out_ref, *, blocks_per_seq):
    blk = q_ref.shape[0]
    n_pairs = q_ref.shape[1] // PAIR
    kv_width = KV_HEADS * HEAD_DIM
    first = (pl.program_id(0) % blocks_per_seq) == 0

    lane = lax.broadcasted_iota(jnp.int32, (1, PAIR), 1)
    head0 = lane < HEAD_DIM
    row = lax.broadcasted_iota(jnp.int32, (2 * blk, 2 * blk), 0)
    col = lax.broadcasted_iota(jnp.int32, (2 * blk, 2 * blk), 1)
    rel = (row & (blk - 1)) - (col - blk)
    valid = (rel >= 0) & (rel < WINDOW) & (jnp.logical_not(first) | (col >= blk))
    top = lax.broadcasted_iota(jnp.int32, (2 * blk, 1), 0) < blk

    kv = jnp.concatenate([kvp_ref[...], kvo_ref[...]], axis=0)

    def dup(x, odd):
        rolled = pltpu.roll(x, HEAD_DIM, 1)
        return jnp.where(head0, rolled, x) if odd else jnp.where(head0, x, rolled)

    for p in range(n_pairs):
        h = (2 * p) // GROUP
        kcol = (h // 2) * PAIR
        kd = dup(kv[:, kcol:kcol + PAIR], h % 2 == 1)
        vd = dup(kv[:, kv_width + kcol:kv_width + kcol + PAIR], h % 2 == 1)
        q = q_ref[:, PAIR * p:PAIR * (p + 1)]
        zero = jnp.zeros_like(q)
        qs = jnp.concatenate([jnp.where(head0, q, zero), jnp.where(head0, zero, q)], axis=0)
        s = jnp.where(valid, _dot_nt(qs, kd), NEG_INF)
        sink = jnp.where(top, sinks_ref[2 * p], sinks_ref[2 * p + 1])
        mx = jnp.maximum(jnp.max(s, axis=-1, keepdims=True), sink)
        pr = jnp.exp(s - mx)
        denom = jnp.sum(pr, axis=-1, keepdims=True) + jnp.exp(sink - mx)
        o2 = _dot(pr.astype(BF16), vd) / denom
        out_ref[:, PAIR * p:PAIR * (p + 1)] = jnp.where(head0, o2[:blk], o2[blk:]).astype(out_ref.dtype)


def _attention(q, kv, sinks, batch, seq):
    m, d = q.shape
    blk = WINDOW
    nb = seq // blk
    return pl.pallas_call(
        functools.partial(_attn_kernel, blocks_per_seq=nb),
        grid=(m // blk,),
        in_specs=[
            pl.BlockSpec(memory_space=pltpu.SMEM),
            pl.BlockSpec((blk, d), lambda i: (i, 0)),
            pl.BlockSpec((blk, kv.shape[1]), lambda i: (jnp.maximum(i - 1, 0), 0)),
            pl.BlockSpec((blk, kv.shape[1]), lambda i: (i, 0)),
        ],
        out_specs=pl.BlockSpec((blk, d), lambda i: (i, 0)),
        out_shape=jax.ShapeDtypeStruct((m, d), BF16),
        compiler_params=pltpu.CompilerParams(
            dimension_semantics=("arbitrary",), vmem_limit_bytes=VMEM_LIMIT),
        name="swa_attention",
    )(sinks, q, kv, kv)


def kernel(x, norm_g, mu, w_rkv, w_decay0, w_decay1, w_decay2, a0, a1, a2, g1, g2, k_k, k_a, r_k, gn_g, gn_b, w_o_rwkv, kv_g, w_kv, w_q, sinks, w_o_attn, w_up, conv_w, conv_b, w_down):
    batch, seq, d = x.shape
    n_rwkv = mu.shape[0]
    depth = norm_g.shape[0]
    bf = lambda t: t.astype(BF16)
    vec = lambda t: t.reshape(1, -1)
    x2 = x.reshape(batch * seq, d)
    kv = None
    for l in range(depth):
        if l < n_rwkv:
            i = l
            parts = _rwkv_pre(
                x2, vec(norm_g[l, 0]), mu[i], bf(w_rkv[i, 0]), bf(w_rkv[i, 1]), bf(w_rkv[i, 2]),
                vec(w_decay0[i]), bf(w_decay1[i]), bf(w_decay2[i]),
                vec(a0[i]), bf(a1[i]), bf(a2[i]), bf(g1[i]), bf(g2[i]),
                vec(k_k[i]), vec(k_a[i]), seq)
            y = _rwkv_scan(*parts, vec(r_k[i]), vec(gn_g[i]), vec(gn_b[i]), batch, seq)
            x2 = _proj_norm_res(y, bf(w_o_rwkv[i]), x2, vec(norm_g[l, 1]))
        else:
            j = l - n_rwkv
            if j == 0:
                q, kv = _attn_pre(x2, vec(norm_g[l, 0]), vec(kv_g), bf(w_q[j]), bf(w_kv))
            else:
                q, _ = _attn_pre(x2, vec(norm_g[l, 0]), vec(kv_g), bf(w_q[j]), bf(w_kv))
            o = _attention(q, kv, sinks[j], batch, seq)
            x2 = _proj_norm_res(o, bf(w_o_attn[j]), x2, vec(norm_g[l, 1]))
        x2 = _conv_ffn(x2, vec(norm_g[l, 2]), bf(w_up[l]), conv_w[l], vec(conv_b[l]),
                       bf(w_down[l]), vec(norm_g[l, 3]), seq)
    return x2.reshape(batch, seq, d)
```

```python
import functools
import math

import jax
import jax.numpy as jnp
from jax import lax
from jax.experimental import pallas as pl
from jax.experimental.pallas import tpu as pltpu

F32 = jnp.float32
BF16 = jnp.bfloat16

HEAD_DIM = 64
KV_HEADS = 4
GROUP = 4
WINDOW = 128
RMS_EPS = 1e-6
GN_EPS = 64e-5
NEG_INF = -1e30

LANES = 128
SUBLANES = 8
BF16_ROWS = 16
PAIR = LANES
CHUNK = 64

VMEM_LIMIT = 56 * 1024 * 1024


def _dot(a, b):
    return jnp.dot(a, b, preferred_element_type=F32)


def _dot_nt(a, b):
    return lax.dot_general(a, b, (((1,), (1,)), ((), ())), preferred_element_type=F32)


def _dot_tn(a, b):
    return lax.dot_general(a, b, (((0,), (0,)), ((), ())), preferred_element_type=F32)


def _rms(x, g):
    return x * lax.rsqrt(jnp.mean(x * x, axis=-1, keepdims=True) + RMS_EPS) * g


def _split_bf16(x):
    hi = x.astype(BF16)
    lo = (x - hi.astype(F32)).astype(BF16)
    return hi, lo


def _div_pow2(x, n):
    assert n & (n - 1) == 0
    return x >> (n.bit_length() - 1)


def _block_ones(n, blk, dtype, value=1.0):
    r = _div_pow2(lax.broadcasted_iota(jnp.int32, (n, n), 0), blk)
    c = _div_pow2(lax.broadcasted_iota(jnp.int32, (n, n), 1), blk)
    return jnp.where(r == c, value, 0.0).astype(dtype)


def _const_spec(shape):
    return pl.BlockSpec(shape, lambda *_: (0,) * len(shape))


def _rwkv_pre_kernel(x_ref, xh_ref, g_ref, mu_ref, wr_ref, wk_ref, wv_ref,
                     w0_ref, w1_ref, w2_ref, a0_ref, a1_ref, a2_ref,
                     g1_ref, g2_ref, kk_ref, ka_ref,
                     r_out, lw_out, k_out, v_out, kk_out, bb_out, g_out,
                     *, tiles_per_seq):
    i = pl.program_id(0)
    g = g_ref[...]
    h = _rms(x_ref[...], g)
    hp = _rms(xh_ref[...], g)
    hp = jnp.where(i % tiles_per_seq == 0, 0.0, hp)
    hcat = jnp.concatenate([hp, h], axis=0)
    prev = pltpu.roll(hcat, 1, 0)[SUBLANES:]
    xx = prev - h

    def mix(j):
        return (h + xx * mu_ref[j:j + 1, :]).astype(BF16)

    r = _dot(mix(0), wr_ref[...])
    z = w0_ref[...] + _dot(jnp.tanh(_dot(mix(1), w1_ref[...])).astype(BF16), w2_ref[...])
    k = _dot(mix(2), wk_ref[...])
    v = _dot(mix(3), wv_ref[...])
    a = jax.nn.sigmoid(a0_ref[...] + _dot(_dot(mix(4), a1_ref[...]).astype(BF16), a2_ref[...]))
    gate = _dot(jax.nn.sigmoid(_dot(mix(5), g1_ref[...])).astype(BF16), g2_ref[...])

    lw = -jax.nn.sigmoid(z) * math.exp(-0.5)

    kkr = k * kk_ref[...]
    sq = kkr * kkr
    d = sq.shape[-1]
    ones4 = _block_ones(2 * PAIR, HEAD_DIM, BF16)
    parts = []
    for p in range(d // (2 * PAIR)):
        hi, lo = _split_bf16(sq[:, 2 * PAIR * p:2 * PAIR * (p + 1)])
        parts.append(_dot(hi, ones4) + _dot(lo, ones4))
    ss = jnp.concatenate(parts, axis=1)
    kk = kkr * lax.rsqrt(jnp.maximum(ss, 1e-24))

    r_out[...] = r
    lw_out[...] = lw
    k_out[...] = k * (1.0 + (a - 1.0) * ka_ref[...])
    v_out[...] = v
    kk_out[...] = kk
    bb_out[...] = kk * a
    g_out[...] = gate


def _rwkv_pre(x2, norm_g, mu, wr, wk, wv, w0, w1, w2, a0, a1, a2, g1, g2, k_k, k_a, seq):
    m, d = x2.shape
    tm = 256
    row = pl.BlockSpec((tm, d), lambda i: (i, 0))
    halo = pl.BlockSpec((SUBLANES, d), lambda i: (jnp.maximum(i * (tm // SUBLANES) - 1, 0), 0))
    ins = [x2, x2, norm_g, mu, wr, wk, wv, w0, w1, w2, a0, a1, a2, g1, g2, k_k, k_a]
    in_specs = [row, halo] + [_const_spec(t.shape) for t in ins[2:]]
    out = jax.ShapeDtypeStruct((m, d), F32)
    return pl.pallas_call(
        functools.partial(_rwkv_pre_kernel, tiles_per_seq=seq // tm),
        grid=(m // tm,),
        in_specs=in_specs,
        out_specs=[row] * 7,
        out_shape=[out] * 7,
        compiler_params=pltpu.CompilerParams(
            dimension_semantics=("arbitrary",), vmem_limit_bytes=VMEM_LIMIT),
        name="rwkv_pre",
    )(*ins)


def _scan_kernel(r_ref, lw_ref, k_ref, v_ref, kk_ref, bb_ref, g_ref,
                 rk_ref, gng_ref, gnb_ref, x_ref, wo_ref, go_ref,
                 out_ref, s_ref, y_ref, *, n_chunks):
    L = CHUNK
    n_seq = r_ref.shape[0]
    n_pairs = r_ref.shape[2] // PAIR

    @pl.when(pl.program_id(1) == 0)
    def _():
        s_ref[...] = jnp.zeros_like(s_ref)

    lane = lax.broadcasted_iota(jnp.int32, (1, PAIR), 1)
    head0 = lane < HEAD_DIM

    def stack2(x):
        return jnp.concatenate([jnp.where(head0, x, 0.0), jnp.where(head0, 0.0, x)], axis=0)

    rr = lax.broadcasted_iota(jnp.int32, (2 * L, 2 * L), 0)
    cc = lax.broadcasted_iota(jnp.int32, (2 * L, 2 * L), 1)
    same_head = _div_pow2(rr, L) == _div_pow2(cc, L)
    eye = rr == cc
    level_masks = []
    s = 4
    while s <= L:
        level_masks.append((_div_pow2(rr, s) == _div_pow2(cc, s))
                           & (_div_pow2(rr, s // 2) != _div_pow2(cc, s // 2)))
        s *= 2
    base_mask = _div_pow2(rr, 2) == _div_pow2(cc, 2)

    tt = lax.broadcasted_iota(jnp.int32, (L, 2 * L), 0)
    ss_ = lax.broadcasted_iota(jnp.int32, (L, 2 * L), 1) & (L - 1)
    strict = tt > ss_
    incl = tt >= ss_
    tri = (lax.broadcasted_iota(jnp.int32, (L, L), 0)
           >= lax.broadcasted_iota(jnp.int32, (L, L), 1)).astype(BF16)

    ones2 = _block_ones(PAIR, HEAD_DIM, BF16)
    mean2 = _block_ones(PAIR, HEAD_DIM, BF16, 1.0 / HEAD_DIM)

    units = [(b, p) for b in range(n_seq) for p in range(n_pairs)]
    pairs = range(len(units))
    sq = [b for b, _ in units]
    col = [slice(PAIR * p, PAIR * (p + 1)) for _, p in units]

    def chunk_body(ci, carry):
        rows = pl.ds(pl.multiple_of(ci * L, L), L)
        ld = lambda ref: [ref[sq[p], rows, col[p]] for p in pairs]

        lw = ld(lw_ref)
        split = [_split_bf16(x) for x in lw]
        cum = [_dot(tri, hi) + _dot(tri, lo) for hi, lo in split]
        total = [c[L - 1:L, :] for c in cum]
        e_neg = [jnp.exp(-c) for c in cum]
        rt = [x * jnp.exp(c) for x, c in zip(ld(r_ref), cum)]
        kt = [x * e for x, e in zip(ld(k_ref), e_neg)]
        at = [-x * jnp.exp(c - w) for x, c, w in zip(ld(kk_ref), cum, lw)]
        bt = [x * e for x, e in zip(ld(bb_ref), e_neg)]

        gmat = [_dot_nt(jnp.concatenate([at[p], rt[p]], axis=0).astype(BF16),
                        jnp.concatenate([stack2(bt[p]), stack2(kt[p])], axis=0).astype(BF16))
                for p in pairs]
        h_ak = [jnp.where(strict, g_[:L, 2 * L:], 0.0) for g_ in gmat]
        h_rb = [jnp.where(incl, g_[L:, :2 * L], 0.0).astype(BF16) for g_ in gmat]
        h_rk = [jnp.where(incl, g_[L:, 2 * L:], 0.0) for g_ in gmat]

        nbd = [stack2(jnp.where(strict, g_[:L, :2 * L], 0.0)) for g_ in gmat]
        tinv = [jnp.where(eye, 1.0, jnp.where(base_mask, n_, 0.0)) for n_ in nbd]
        for lm in level_masks:
            tb = [t_.astype(BF16) for t_ in tinv]
            x = [_dot(tb[p], jnp.where(lm, nbd[p], 0.0).astype(BF16)) for p in pairs]
            tinv = [tinv[p] + _dot(x[p].astype(BF16), tb[p]) for p in pairs]
        th = [(t_[:L] + t_[L:]).astype(BF16) for t_ in tinv]

        v = ld(v_ref)
        zy = [_dot(jnp.concatenate([h_ak[p], h_rk[p]], axis=0).astype(BF16),
                   stack2(v[p]).astype(BF16)) for p in pairs]
        aw = [_dot(th[p], jnp.concatenate([stack2(at[p]), stack2(zy[p][:L])], axis=1).astype(BF16))
              for p in pairs]

        state = [s_ref[p] for p in pairs]
        uy = [_dot_nt(jnp.concatenate([aw[p][:, :PAIR], rt[p]], axis=0).astype(BF16),
                      state[p].astype(BF16)) for p in pairs]
        u = [uy[p][:L] + aw[p][:, PAIR:] for p in pairs]
        y = [uy[p][L:] + zy[p][L:] + _dot(h_rb[p], stack2(u[p]).astype(BF16)) for p in pairs]

        k = ld(k_ref)
        bb = ld(bb_ref)
        e_end = [jnp.exp(total[p] - cum[p]) for p in pairs]
        upd = [_dot_tn(jnp.concatenate([u[p], v[p]], axis=0).astype(BF16),
                       jnp.concatenate([bb[p] * e_end[p], k[p] * e_end[p]], axis=0).astype(BF16))
               for p in pairs]
        for p in pairs:
            s_ref[p] = state[p] * jnp.exp(total[p]) + jnp.where(same_head, upd[p], 0.0)

        ysplit = [_split_bf16(y_) for y_ in y]
        yc = [y[p] - (_dot(ysplit[p][0], mean2) + _dot(ysplit[p][1], mean2)) for p in pairs]
        var = [_dot((c * c).astype(BF16), mean2) for c in yc]
        r = ld(r_ref)
        bonus = [_dot((r[p] * k[p] * rk_ref[:, col[p]]).astype(BF16), ones2) * v[p] for p in pairs]
        for p in pairs:
            yn = yc[p] * lax.rsqrt(var[p] + GN_EPS) * gng_ref[:, col[p]] + gnb_ref[:, col[p]]
            y_ref[sq[p], rows, col[p]] = ((yn + bonus[p]) * g_ref[sq[p], rows, col[p]]).astype(y_ref.dtype)
        return carry

    lax.fori_loop(0, n_chunks, chunk_body, 0)

    for b in range(n_seq):
        out_ref[b] = x_ref[b] + _rms(_dot(y_ref[b], wo_ref[...]), go_ref[...])


def _rwkv_scan(r, lw, k, v, kk, bb, g, r_k, gn_g, gn_b, x2, w_o, g_o, batch, seq):
    m, d = r.shape
    tt = 256
    n_seq = 2
    steps = seq // tt
    as4d = lambda t: t.reshape(batch // n_seq, n_seq, seq, d)
    row = pl.BlockSpec((None, n_seq, tt, d), lambda b, c: (b, 0, c, 0))
    vec = pl.BlockSpec((1, d), lambda b, c: (0, 0))
    out = pl.pallas_call(
        functools.partial(_scan_kernel, n_chunks=tt // CHUNK),
        grid=(batch // n_seq, steps),
        in_specs=[row] * 7 + [vec] * 3
        + [row, pl.BlockSpec(w_o.shape, lambda b, c: (0, 0), pipeline_mode=pl.Buffered(1)), vec],
        out_specs=row,
        out_shape=jax.ShapeDtypeStruct((batch // n_seq, n_seq, seq, d), F32),
        scratch_shapes=[pltpu.VMEM((n_seq * (d // PAIR), PAIR, PAIR), F32),
                        pltpu.VMEM((n_seq, tt, d), BF16)],
        compiler_params=pltpu.CompilerParams(
            dimension_semantics=("arbitrary", "arbitrary"), vmem_limit_bytes=VMEM_LIMIT),
        name="rwkv_scan",
    )(*[as4d(t) for t in (r, lw, k, v, kk, bb, g)], r_k, gn_g, gn_b, as4d(x2), w_o, g_o)
    return out.reshape(m, d)


def _proj_norm_res_kernel(y_ref, w_ref, x_ref, g_ref, out_ref):
    h = _dot(y_ref[...], w_ref[...])
    out_ref[...] = x_ref[...] + _rms(h, g_ref[...])


def _proj_norm_res(y, w, x2, g):
    m, d = x2.shape
    tm = 512
    row_y = pl.BlockSpec((tm, y.shape[1]), lambda i: (i, 0))
    row_x = pl.BlockSpec((tm, d), lambda i: (i, 0))
    return pl.pallas_call(
        _proj_norm_res_kernel,
        grid=(m // tm,),
        in_specs=[row_y, _const_spec(w.shape), row_x, _const_spec(g.shape)],
        out_specs=row_x,
        out_shape=jax.ShapeDtypeStruct((m, d), F32),
        compiler_params=pltpu.CompilerParams(
            dimension_semantics=("arbitrary",), vmem_limit_bytes=VMEM_LIMIT),
        name="proj_norm_res",
    )(y, w, x2, g)


def _ffn_kernel(x_ref, xh_ref, gin_ref, wup_ref, cw_ref, cb_ref, wdn_ref, gout_ref,
                out_ref, xcat_ref, h_ref, act_ref, acc_ref, *, tiles_per_seq, d_ff, f_chunk):
    i = pl.program_id(0)
    halo = xh_ref.shape[0]
    tm = x_ref.shape[0]
    gin = gin_ref[...]
    xp = _rms(xh_ref[...], gin)
    xcat_ref[:halo, :] = jnp.where(i % tiles_per_seq == 0, 0.0, xp).astype(BF16)
    xcat_ref[halo:, :] = _rms(x_ref[...], gin).astype(BF16)
    c_gelu = math.sqrt(2.0 / math.pi)
    n_chunks = d_ff // f_chunk

    def up(j):
        for half in range(2):
            c0 = half * d_ff + j * f_chunk
            h_ref[j % 2, half] = _dot(xcat_ref[...], wup_ref[:, c0:c0 + f_chunk])

    def conv(j, half):
        col = slice(half * d_ff + j * f_chunk, half * d_ff + (j + 1) * f_chunk)
        w = cw_ref[:, col]
        return (h_ref[j % 2, half, pl.ds(halo, tm), :] * w[2:3]
                + h_ref[j % 2, half, pl.ds(halo - 1, tm), :] * w[1:2]
                + h_ref[j % 2, half, pl.ds(halo - 2, tm), :] * w[0:1] + cb_ref[:, col])

    def down(j):
        part = _dot(act_ref[j % 2], wdn_ref[j * f_chunk:(j + 1) * f_chunk, :])
        if j == 0:
            acc_ref[...] = part
        else:
            acc_ref[...] += part

    up(0)
    for j in range(n_chunks):
        if j + 1 < n_chunks:
            up(j + 1)
        if j > 0:
            down(j - 1)
        a = conv(j, 0)
        u = conv(j, 1)
        gelu = a * (0.5 * (1.0 + jnp.tanh(c_gelu * (a + 0.044715 * (a * a * a)))))
        act_ref[j % 2] = (gelu * u).astype(BF16)
    down(n_chunks - 1)
    out_ref[...] = x_ref[...] + _rms(acc_ref[...], gout_ref[...])


def _conv_ffn(x2, g_in, w_up, conv_w, conv_b, w_down, g_out, seq):
    m, d = x2.shape
    d_ff = w_down.shape[0]
    tm = 512
    f_chunk = 256
    row = pl.BlockSpec((tm, d), lambda i: (i, 0))
    halo = pl.BlockSpec((BF16_ROWS, d), lambda i: (jnp.maximum(i * (tm // BF16_ROWS) - 1, 0), 0))
    single = pl.Buffered(1)
    in_specs = [
        row, halo, _const_spec(g_in.shape),
        pl.BlockSpec(w_up.shape, lambda i: (0, 0), pipeline_mode=single),
        _const_spec(conv_w.shape), _const_spec(conv_b.shape),
        pl.BlockSpec(w_down.shape, lambda i: (0, 0), pipeline_mode=single),
        _const_spec(g_out.shape),
    ]
    return pl.pallas_call(
        functools.partial(_ffn_kernel, tiles_per_seq=seq // tm, d_ff=d_ff, f_chunk=f_chunk),
        grid=(m // tm,),
        in_specs=in_specs,
        out_specs=row,
        out_shape=jax.ShapeDtypeStruct((m, d), F32),
        scratch_shapes=[pltpu.VMEM((BF16_ROWS + tm, d), BF16),
                        pltpu.VMEM((2, 2, BF16_ROWS + tm, f_chunk), F32),
                        pltpu.VMEM((2, tm, f_chunk), BF16),
                        pltpu.VMEM((tm, d), F32)],
        compiler_params=pltpu.CompilerParams(
            dimension_semantics=("arbitrary",), vmem_limit_bytes=VMEM_LIMIT),
        name="conv_ffn",
    )(x2, x2, g_in, w_up, conv_w, conv_b, w_down, g_out)


def _attn_pre_kernel(x_ref, gq_ref, gkv_ref, wq_ref, wkv_ref, q_out, kv_out):
    x = x_ref[...]
    xn = x * lax.rsqrt(jnp.mean(x * x, axis=-1, keepdims=True) + RMS_EPS)
    q = _dot((xn * gq_ref[...]).astype(BF16), wq_ref[...])
    q_out[...] = (q * (HEAD_DIM ** -0.5)).astype(q_out.dtype)
    kv_out[...] = _dot((xn * gkv_ref[...]).astype(BF16), wkv_ref[...]).astype(kv_out.dtype)


def _attn_pre(x2, g_q, g_kv, w_q, w_kv):
    m, d = x2.shape
    tm = 512
    row = pl.BlockSpec((tm, d), lambda i: (i, 0))
    return pl.pallas_call(
        _attn_pre_kernel,
        grid=(m // tm,),
        in_specs=[row, _const_spec(g_q.shape), _const_spec(g_kv.shape),
                  _const_spec(w_q.shape), _const_spec(w_kv.shape)],
        out_specs=[pl.BlockSpec((tm, w_q.shape[1]), lambda i: (i, 0)),
                   pl.BlockSpec((tm, w_kv.shape[1]), lambda i: (i, 0))],
        out_shape=[jax.ShapeDtypeStruct((m, w_q.shape[1]), BF16),
                   jax.ShapeDtypeStruct((m, w_kv.shape[1]), BF16)],
        compiler_params=pltpu.CompilerParams(
            dimension_semantics=("arbitrary",), vmem_limit_bytes=VMEM_LIMIT),
        name="attn_pre",
    )(x2, g_q, g_kv, w_q, w_kv)


def _attn_kernel(sinks_ref, q_ref, kvp_ref, kvo_ref, out_ref, *, blocks_per_seq):
    blk = q_ref.shape[0]
    n_pairs = q_ref.shape[1] // PAIR
    kv_width = KV_HEADS * HEAD_DIM
    first = (pl.program_id(0) % blocks_per_seq) == 0

    lane = lax.broadcasted_iota(jnp.int32, (1, PAIR), 1)
    head0 = lane < HEAD_DIM
    row = lax.broadcasted_iota(jnp.int32, (2 * blk, 2 * blk), 0)
    col = lax.broadcasted_iota(jnp.int32, (2 * blk, 2 * blk), 1)
    rel = (row & (blk - 1)) - (col - blk)
    valid = (rel >= 0) & (rel < WINDOW) & (jnp.logical_not(first) | (col >= blk))
    top = lax.broadcasted_iota(jnp.int32, (2 * blk, 1), 0) < blk

    kv = jnp.concatenate([kvp_ref[...], kvo_ref[...]], axis=0)

    def dup(x, odd):
        rolled = pltpu.roll(x, HEAD_DIM, 1)
        return jnp.where(head0, rolled, x) if odd else jnp.where(head0, x, rolled)

    pairs = range(n_pairs)
    kd, vd = [], []
    for h in range(KV_HEADS):
        kcol = (h // 2) * PAIR
        kd.append(dup(kv[:, kcol:kcol + PAIR], h % 2 == 1))
        vd.append(dup(kv[:, kv_width + kcol:kv_width + kcol + PAIR], h % 2 == 1))

    def qstack(p):
        q = q_ref[:, PAIR * p:PAIR * (p + 1)]
        zero = jnp.zeros_like(q)
        return jnp.concatenate([jnp.where(head0, q, zero), jnp.where(head0, zero, q)], axis=0)

    s = [jnp.where(valid, _dot_nt(qstack(p), kd[(2 * p) // GROUP]), NEG_INF) for p in pairs]
    sink = [jnp.where(top, sinks_ref[2 * p], sinks_ref[2 * p + 1]) for p in pairs]
    mx = [jnp.maximum(jnp.max(s[p], axis=-1, keepdims=True), sink[p]) for p in pairs]
    pr = [jnp.exp(s[p] - mx[p]) for p in pairs]
    denom = [jnp.sum(pr[p], axis=-1, keepdims=True) + jnp.exp(sink[p] - mx[p]) for p in pairs]
    o2 = [_dot(pr[p].astype(BF16), vd[(2 * p) // GROUP]) / denom[p] for p in pairs]
    for p in pairs:
        out_ref[:, PAIR * p:PAIR * (p + 1)] = jnp.where(
            head0, o2[p][:blk], o2[p][blk:]).astype(out_ref.dtype)


def _attention(q, kv, sinks, batch, seq):
    m, d = q.shape
    blk = WINDOW
    nb = seq // blk
    return pl.pallas_call(
        functools.partial(_attn_kernel, blocks_per_seq=nb),
        grid=(m // blk,),
        in_specs=[
            pl.BlockSpec(memory_space=pltpu.SMEM),
            pl.BlockSpec((blk, d), lambda i: (i, 0)),
            pl.BlockSpec((blk, kv.shape[1]), lambda i: (jnp.maximum(i - 1, 0), 0)),
            pl.BlockSpec((blk, kv.shape[1]), lambda i: (i, 0)),
        ],
        out_specs=pl.BlockSpec((blk, d), lambda i: (i, 0)),
        out_shape=jax.ShapeDtypeStruct((m, d), BF16),
        compiler_params=pltpu.CompilerParams(
            dimension_semantics=("arbitrary",), vmem_limit_bytes=VMEM_LIMIT),
        name="swa_attention",
    )(sinks, q, kv, kv)


def kernel(x, norm_g, mu, w_rkv, w_decay0, w_decay1, w_decay2, a0, a1, a2, g1, g2, k_k, k_a, r_k, gn_g, gn_b, w_o_rwkv, kv_g, w_kv, w_q, sinks, w_o_attn, w_up, conv_w, conv_b, w_down):
    batch, seq, d = x.shape
    n_rwkv = mu.shape[0]
    depth = norm_g.shape[0]
    bf = lambda t: t.astype(BF16)
    vec = lambda t: t.reshape(1, -1)
    x2 = x.reshape(batch * seq, d)
    kv = None
    for l in range(depth):
        if l < n_rwkv:
            i = l
            parts = _rwkv_pre(
                x2, vec(norm_g[l, 0]), mu[i], bf(w_rkv[i, 0]), bf(w_rkv[i, 1]), bf(w_rkv[i, 2]),
                vec(w_decay0[i]), bf(w_decay1[i]), bf(w_decay2[i]),
                vec(a0[i]), bf(a1[i]), bf(a2[i]), bf(g1[i]), bf(g2[i]),
                vec(k_k[i]), vec(k_a[i]), seq)
            x2 = _rwkv_scan(*parts, vec(r_k[i]), vec(gn_g[i]), vec(gn_b[i]),
                            x2, bf(w_o_rwkv[i]), vec(norm_g[l, 1]), batch, seq)
        else:
            j = l - n_rwkv
            if j == 0:
                q, kv = _attn_pre(x2, vec(norm_g[l, 0]), vec(kv_g), bf(w_q[j]), bf(w_kv))
            else:
                q, _ = _attn_pre(x2, vec(norm_g[l, 0]), vec(kv_g), bf(w_q[j]), bf(w_kv))
            o = _attention(q, kv, sinks[j], batch, seq)
            x2 = _proj_norm_res(o, bf(w_o_attn[j]), x2, vec(norm_g[l, 1]))
        x2 = _conv_ffn(x2, vec(norm_g[l, 2]), bf(w_up[l]), conv_w[l], vec(conv_b[l]),
                       bf(w_down[l]), vec(norm_g[l, 3]), seq)
    return x2.reshape(batch, seq, d)
```

```python
import functools
import math

import jax
import jax.numpy as jnp
from jax import lax
from jax.experimental import pallas as pl
from jax.experimental.pallas import tpu as pltpu

F32 = jnp.float32
BF16 = jnp.bfloat16

HEAD_DIM = 64
KV_HEADS = 4
GROUP = 4
WINDOW = 128
RMS_EPS = 1e-6
GN_EPS = 64e-5
NEG_INF = -1e30

LANES = 128
SUBLANES = 8
BF16_ROWS = 16
PAIR = LANES
CHUNK = 64

VMEM_LIMIT = 56 * 1024 * 1024


def _dot(a, b):
    return jnp.dot(a, b, preferred_element_type=F32)


def _dot_nt(a, b):
    return lax.dot_general(a, b, (((1,), (1,)), ((), ())), preferred_element_type=F32)


def _dot_tn(a, b):
    return lax.dot_general(a, b, (((0,), (0,)), ((), ())), preferred_element_type=F32)


def _rms(x, g):
    return x * lax.rsqrt(jnp.mean(x * x, axis=-1, keepdims=True) + RMS_EPS) * g


def _split_bf16(x):
    hi = x.astype(BF16)
    lo = (x - hi.astype(F32)).astype(BF16)
    return hi, lo


def _div_pow2(x, n):
    assert n & (n - 1) == 0
    return x >> (n.bit_length() - 1)


def _block_ones(n, blk, dtype, value=1.0):
    r = _div_pow2(lax.broadcasted_iota(jnp.int32, (n, n), 0), blk)
    c = _div_pow2(lax.broadcasted_iota(jnp.int32, (n, n), 1), blk)
    return jnp.where(r == c, value, 0.0).astype(dtype)


def _const_spec(shape):
    return pl.BlockSpec(shape, lambda *_: (0,) * len(shape))


def _rwkv_pre_kernel(x_ref, xh_ref, g_ref, mu_ref, wr_ref, wk_ref, wv_ref,
                     w0_ref, w1_ref, w2_ref, a0_ref, a1_ref, a2_ref,
                     g1_ref, g2_ref, kk_ref, ka_ref,
                     r_out, lw_out, k_out, v_out, kk_out, bb_out, g_out,
                     *, tiles_per_seq):
    i = pl.program_id(0)
    g = g_ref[...]
    h = _rms(x_ref[...], g)
    hp = _rms(xh_ref[...], g)
    hp = jnp.where(i % tiles_per_seq == 0, 0.0, hp)
    hcat = jnp.concatenate([hp, h], axis=0)
    prev = pltpu.roll(hcat, 1, 0)[SUBLANES:]
    xx = prev - h

    def mix(j):
        return (h + xx * mu_ref[j:j + 1, :]).astype(BF16)

    w_lo = _dot(mix(1), w1_ref[...])
    a_lo = _dot(mix(4), a1_ref[...])
    g_lo = _dot(mix(5), g1_ref[...])
    r = _dot(mix(0), wr_ref[...])
    k = _dot(mix(2), wk_ref[...])
    v = _dot(mix(3), wv_ref[...])
    z = w0_ref[...] + _dot(jnp.tanh(w_lo).astype(BF16), w2_ref[...])
    a = jax.nn.sigmoid(a0_ref[...] + _dot(a_lo.astype(BF16), a2_ref[...]))
    gate = _dot(jax.nn.sigmoid(g_lo).astype(BF16), g2_ref[...])

    lw = -jax.nn.sigmoid(z) * math.exp(-0.5)

    kkr = k * kk_ref[...]
    sq = kkr * kkr
    d = sq.shape[-1]
    ones4 = _block_ones(2 * PAIR, HEAD_DIM, BF16)
    parts = []
    for p in range(d // (2 * PAIR)):
        hi, lo = _split_bf16(sq[:, 2 * PAIR * p:2 * PAIR * (p + 1)])
        parts.append(_dot(hi, ones4) + _dot(lo, ones4))
    ss = jnp.concatenate(parts, axis=1)
    kk = kkr * lax.rsqrt(jnp.maximum(ss, 1e-24))

    r_out[...] = r.astype(r_out.dtype)
    lw_out[...] = lw
    k_out[...] = (k * (1.0 + (a - 1.0) * ka_ref[...])).astype(k_out.dtype)
    v_out[...] = v.astype(v_out.dtype)
    kk_out[...] = kk.astype(kk_out.dtype)
    bb_out[...] = (kk * a).astype(bb_out.dtype)
    g_out[...] = gate.astype(g_out.dtype)


def _rwkv_pre(x2, norm_g, mu, wr, wk, wv, w0, w1, w2, a0, a1, a2, g1, g2, k_k, k_a, seq):
    m, d = x2.shape
    tm = 256
    row = pl.BlockSpec((tm, d), lambda i: (i, 0))
    halo = pl.BlockSpec((SUBLANES, d), lambda i: (jnp.maximum(i * (tm // SUBLANES) - 1, 0), 0))
    ins = [x2, x2, norm_g, mu, wr, wk, wv, w0, w1, w2, a0, a1, a2, g1, g2, k_k, k_a]
    in_specs = [row, halo] + [_const_spec(t.shape) for t in ins[2:]]
    out = [jax.ShapeDtypeStruct((m, d), F32 if j == 1 else BF16) for j in range(7)]
    return pl.pallas_call(
        functools.partial(_rwkv_pre_kernel, tiles_per_seq=seq // tm),
        grid=(m // tm,),
        in_specs=in_specs,
        out_specs=[row] * 7,
        out_shape=out,
        compiler_params=pltpu.CompilerParams(
            dimension_semantics=("arbitrary",), vmem_limit_bytes=VMEM_LIMIT),
        name="rwkv_pre",
    )(*ins)


def _scan_kernel(r_ref, lw_ref, k_ref, v_ref, kk_ref, bb_ref, g_ref,
                 rk_ref, gng_ref, gnb_ref, x_ref, wo_ref, go_ref,
                 out_ref, s_ref, y_ref, *, n_chunks):
    L = CHUNK
    n_seq = r_ref.shape[0]
    n_pairs = r_ref.shape[2] // PAIR

    @pl.when(pl.program_id(1) == 0)
    def _():
        s_ref[...] = jnp.zeros_like(s_ref)

    lane = lax.broadcasted_iota(jnp.int32, (1, PAIR), 1)
    head0 = lane < HEAD_DIM

    def stack2(x):
        return jnp.concatenate([jnp.where(head0, x, 0.0), jnp.where(head0, 0.0, x)], axis=0)

    rr = lax.broadcasted_iota(jnp.int32, (2 * L, 2 * L), 0)
    cc = lax.broadcasted_iota(jnp.int32, (2 * L, 2 * L), 1)
    same_head = _div_pow2(rr, L) == _div_pow2(cc, L)
    eye = rr == cc
    level_masks = []
    s = 4
    while s <= L:
        level_masks.append((_div_pow2(rr, s) == _div_pow2(cc, s))
                           & (_div_pow2(rr, s // 2) != _div_pow2(cc, s // 2)))
        s *= 2
    base_mask = _div_pow2(rr, 2) == _div_pow2(cc, 2)

    tt = lax.broadcasted_iota(jnp.int32, (L, 2 * L), 0)
    ss_ = lax.broadcasted_iota(jnp.int32, (L, 2 * L), 1) & (L - 1)
    strict = tt > ss_
    incl = tt >= ss_
    tri = (lax.broadcasted_iota(jnp.int32, (L, L), 0)
           >= lax.broadcasted_iota(jnp.int32, (L, L), 1)).astype(BF16)

    ones2 = _block_ones(PAIR, HEAD_DIM, BF16)
    mean2 = _block_ones(PAIR, HEAD_DIM, BF16, 1.0 / HEAD_DIM)

    units = [(b, p) for b in range(n_seq) for p in range(n_pairs)]
    pairs = range(len(units))
    sq = [b for b, _ in units]
    col = [slice(PAIR * p, PAIR * (p + 1)) for _, p in units]

    def chunk_body(ci, carry):
        rows = pl.ds(pl.multiple_of(ci * L, L), L)
        ld = lambda ref: [ref[sq[p], rows, col[p]].astype(F32) for p in pairs]

        lw = ld(lw_ref)
        split = [_split_bf16(x) for x in lw]
        cum = [_dot(tri, hi) + _dot(tri, lo) for hi, lo in split]
        total = [c[L - 1:L, :] for c in cum]
        e_neg = [jnp.exp(-c) for c in cum]
        rt = [x * jnp.exp(c) for x, c in zip(ld(r_ref), cum)]
        kt = [x * e for x, e in zip(ld(k_ref), e_neg)]
        at = [-x * jnp.exp(c - w) for x, c, w in zip(ld(kk_ref), cum, lw)]
        bt = [x * e for x, e in zip(ld(bb_ref), e_neg)]

        gmat = [_dot_nt(jnp.concatenate([at[p], rt[p]], axis=0).astype(BF16),
                        jnp.concatenate([stack2(bt[p]), stack2(kt[p])], axis=0).astype(BF16))
                for p in pairs]
        h_ak = [jnp.where(strict, g_[:L, 2 * L:], 0.0) for g_ in gmat]
        h_rb = [jnp.where(incl, g_[L:, :2 * L], 0.0).astype(BF16) for g_ in gmat]
        h_rk = [jnp.where(incl, g_[L:, 2 * L:], 0.0) for g_ in gmat]

        nbd = [stack2(jnp.where(strict, g_[:L, :2 * L], 0.0)) for g_ in gmat]
        tinv = [jnp.where(eye, 1.0, jnp.where(base_mask, n_, 0.0)) for n_ in nbd]
        for lm in level_masks:
            tb = [t_.astype(BF16) for t_ in tinv]
            x = [_dot(tb[p], jnp.where(lm, nbd[p], 0.0).astype(BF16)) for p in pairs]
            tinv = [tinv[p] + _dot(x[p].astype(BF16), tb[p]) for p in pairs]
        th = [(t_[:L] + t_[L:]).astype(BF16) for t_ in tinv]

        v = ld(v_ref)
        zy = [_dot(jnp.concatenate([h_ak[p], h_rk[p]], axis=0).astype(BF16),
                   stack2(v[p]).astype(BF16)) for p in pairs]
        aw = [_dot(th[p], jnp.concatenate([stack2(at[p]), stack2(zy[p][:L])], axis=1).astype(BF16))
              for p in pairs]

        state = [s_ref[p] for p in pairs]
        uy = [_dot_nt(jnp.concatenate([aw[p][:, :PAIR], rt[p]], axis=0).astype(BF16),
                      state[p].astype(BF16)) for p in pairs]
        u = [uy[p][:L] + aw[p][:, PAIR:] for p in pairs]
        y = [uy[p][L:] + zy[p][L:] + _dot(h_rb[p], stack2(u[p]).astype(BF16)) for p in pairs]

        k = ld(k_ref)
        bb = ld(bb_ref)
        e_end = [jnp.exp(total[p] - cum[p]) for p in pairs]
        upd = [_dot_tn(jnp.concatenate([u[p], v[p]], axis=0).astype(BF16),
                       jnp.concatenate([bb[p] * e_end[p], k[p] * e_end[p]], axis=0).astype(BF16))
               for p in pairs]
        for p in pairs:
            s_ref[p] = state[p] * jnp.exp(total[p]) + jnp.where(same_head, upd[p], 0.0)

        ysplit = [_split_bf16(y_) for y_ in y]
        yc = [y[p] - (_dot(ysplit[p][0], mean2) + _dot(ysplit[p][1], mean2)) for p in pairs]
        var = [_dot((c * c).astype(BF16), mean2) for c in yc]
        r = ld(r_ref)
        bonus = [_dot((r[p] * k[p] * rk_ref[:, col[p]]).astype(BF16), ones2) * v[p] for p in pairs]
        for p in pairs:
            yn = yc[p] * lax.rsqrt(var[p] + GN_EPS) * gng_ref[:, col[p]] + gnb_ref[:, col[p]]
            y_ref[sq[p], rows, col[p]] = ((yn + bonus[p]) * g_ref[sq[p], rows, col[p]]).astype(y_ref.dtype)
        return carry

    lax.fori_loop(0, n_chunks, chunk_body, 0)

    for b in range(n_seq):
        out_ref[b] = x_ref[b] + _rms(_dot(y_ref[b], wo_ref[...]), go_ref[...])


def _rwkv_scan(r, lw, k, v, kk, bb, g, r_k, gn_g, gn_b, x2, w_o, g_o, batch, seq):
    m, d = r.shape
    tt = 256
    n_seq = 2
    steps = seq // tt
    as4d = lambda t: t.reshape(batch // n_seq, n_seq, seq, d)
    row = pl.BlockSpec((None, n_seq, tt, d), lambda b, c: (b, 0, c, 0))
    vec = pl.BlockSpec((1, d), lambda b, c: (0, 0))
    out = pl.pallas_call(
        functools.partial(_scan_kernel, n_chunks=tt // CHUNK),
        grid=(batch // n_seq, steps),
        in_specs=[row] * 7 + [vec] * 3
        + [row, pl.BlockSpec(w_o.shape, lambda b, c: (0, 0), pipeline_mode=pl.Buffered(1)), vec],
        out_specs=row,
        out_shape=jax.ShapeDtypeStruct((batch // n_seq, n_seq, seq, d), F32),
        scratch_shapes=[pltpu.VMEM((n_seq * (d // PAIR), PAIR, PAIR), F32),
                        pltpu.VMEM((n_seq, tt, d), BF16)],
        compiler_params=pltpu.CompilerParams(
            dimension_semantics=("arbitrary", "arbitrary"), vmem_limit_bytes=VMEM_LIMIT),
        name="rwkv_scan",
    )(*[as4d(t) for t in (r, lw, k, v, kk, bb, g)], r_k, gn_g, gn_b, as4d(x2), w_o, g_o)
    return out.reshape(m, d)


def _ffn_kernel(x_ref, xh_ref, gin_ref, wup_ref, cw_ref, cb_ref, wdn_ref, gout_ref,
                out_ref, xcat_ref, h_ref, act_ref, acc_ref, *, tiles_per_seq, d_ff, f_chunk):
    i = pl.program_id(0)
    halo = xh_ref.shape[0]
    tm = x_ref.shape[0]
    gin = gin_ref[...]
    xp = _rms(xh_ref[...], gin)
    xcat_ref[:halo, :] = jnp.where(i % tiles_per_seq == 0, 0.0, xp).astype(BF16)
    xcat_ref[halo:, :] = _rms(x_ref[...], gin).astype(BF16)
    c_gelu = math.sqrt(2.0 / math.pi)
    n_chunks = d_ff // f_chunk

    def up(j):
        for half in range(2):
            c0 = half * d_ff + j * f_chunk
            h_ref[j % 2, half] = _dot(xcat_ref[...], wup_ref[:, c0:c0 + f_chunk])

    def conv(j, half):
        col = slice(half * d_ff + j * f_chunk, half * d_ff + (j + 1) * f_chunk)
        w = cw_ref[:, col]
        return (h_ref[j % 2, half, pl.ds(halo, tm), :] * w[2:3]
                + h_ref[j % 2, half, pl.ds(halo - 1, tm), :] * w[1:2]
                + h_ref[j % 2, half, pl.ds(halo - 2, tm), :] * w[0:1] + cb_ref[:, col])

    def down(j):
        part = _dot(act_ref[j % 2], wdn_ref[j * f_chunk:(j + 1) * f_chunk, :])
        if j == 0:
            acc_ref[...] = part
        else:
            acc_ref[...] += part

    up(0)
    for j in range(n_chunks):
        if j + 1 < n_chunks:
            up(j + 1)
        if j > 0:
            down(j - 1)
        a = conv(j, 0)
        u = conv(j, 1)
        gelu = a * (0.5 * (1.0 + jnp.tanh(c_gelu * (a + 0.044715 * (a * a * a)))))
        act_ref[j % 2] = (gelu * u).astype(BF16)
    down(n_chunks - 1)
    out_ref[...] = x_ref[...] + _rms(acc_ref[...], gout_ref[...])


def _conv_ffn(x2, g_in, w_up, conv_w, conv_b, w_down, g_out, seq):
    m, d = x2.shape
    d_ff = w_down.shape[0]
    tm = 512
    f_chunk = 256
    row = pl.BlockSpec((tm, d), lambda i: (i, 0))
    halo = pl.BlockSpec((BF16_ROWS, d), lambda i: (jnp.maximum(i * (tm // BF16_ROWS) - 1, 0), 0))
    single = pl.Buffered(1)
    in_specs = [
        row, halo, _const_spec(g_in.shape),
        pl.BlockSpec(w_up.shape, lambda i: (0, 0), pipeline_mode=single),
        _const_spec(conv_w.shape), _const_spec(conv_b.shape),
        pl.BlockSpec(w_down.shape, lambda i: (0, 0), pipeline_mode=single),
        _const_spec(g_out.shape),
    ]
    return pl.pallas_call(
        functools.partial(_ffn_kernel, tiles_per_seq=seq // tm, d_ff=d_ff, f_chunk=f_chunk),
        grid=(m // tm,),
        in_specs=in_specs,
        out_specs=row,
        out_shape=jax.ShapeDtypeStruct((m, d), F32),
        scratch_shapes=[pltpu.VMEM((BF16_ROWS + tm, d), BF16),
                        pltpu.VMEM((2, 2, BF16_ROWS + tm, f_chunk), F32),
                        pltpu.VMEM((2, tm, f_chunk), BF16),
                        pltpu.VMEM((tm, d), F32)],
        compiler_params=pltpu.CompilerParams(
            dimension_semantics=("arbitrary",), vmem_limit_bytes=VMEM_LIMIT),
        name="conv_ffn",
    )(x2, x2, g_in, w_up, conv_w, conv_b, w_down, g_out)


def _attn_pre_kernel(x_ref, gq_ref, gkv_ref, wq_ref, wkv_ref, q_out, kv_out):
    x = x_ref[...]
    xn = x * lax.rsqrt(jnp.mean(x * x, axis=-1, keepdims=True) + RMS_EPS)
    q = _dot((xn * gq_ref[...]).astype(BF16), wq_ref[...])
    q_out[...] = (q * (HEAD_DIM ** -0.5)).astype(q_out.dtype)
    kv_out[...] = _dot((xn * gkv_ref[...]).astype(BF16), wkv_ref[...]).astype(kv_out.dtype)


def _attn_pre(x2, g_q, g_kv, w_q, w_kv):
    m, d = x2.shape
    tm = 512
    row = pl.BlockSpec((tm, d), lambda i: (i, 0))
    return pl.pallas_call(
        _attn_pre_kernel,
        grid=(m // tm,),
        in_specs=[row, _const_spec(g_q.shape), _const_spec(g_kv.shape),
                  _const_spec(w_q.shape), _const_spec(w_kv.shape)],
        out_specs=[pl.BlockSpec((tm, w_q.shape[1]), lambda i: (i, 0)),
                   pl.BlockSpec((tm, w_kv.shape[1]), lambda i: (i, 0))],
        out_shape=[jax.ShapeDtypeStruct((m, w_q.shape[1]), BF16),
                   jax.ShapeDtypeStruct((m, w_kv.shape[1]), BF16)],
        compiler_params=pltpu.CompilerParams(
            dimension_semantics=("arbitrary",), vmem_limit_bytes=VMEM_LIMIT),
        name="attn_pre",
    )(x2, g_q, g_kv, w_q, w_kv)


def _attn_kernel(sinks_ref, q_ref, kvp_ref, kvo_ref, x_ref, wo_ref, go_ref,
                 out_ref, o_ref, *, steps_per_seq):
    blk = kvp_ref.shape[0]
    n_q = q_ref.shape[0] // blk
    n_pairs = q_ref.shape[1] // PAIR
    kv_width = KV_HEADS * HEAD_DIM
    first = (pl.program_id(0) % steps_per_seq) == 0

    lane = lax.broadcasted_iota(jnp.int32, (1, PAIR), 1)
    head0 = lane < HEAD_DIM
    row = lax.broadcasted_iota(jnp.int32, (2 * blk, 2 * blk), 0)
    col = lax.broadcasted_iota(jnp.int32, (2 * blk, 2 * blk), 1)
    rel = (row & (blk - 1)) - (col - blk)
    in_window = (rel >= 0) & (rel < WINDOW)
    valid = [in_window & (jnp.logical_not(first) | (col >= blk))] + [in_window] * (n_q - 1)
    top = lax.broadcasted_iota(jnp.int32, (2 * blk, 1), 0) < blk

    kv = jnp.concatenate([kvp_ref[...], kvo_ref[...]], axis=0)

    def dup(x, odd):
        rolled = pltpu.roll(x, HEAD_DIM, 1)
        return jnp.where(head0, rolled, x) if odd else jnp.where(head0, x, rolled)

    kd, vd = {}, {}
    for b in range(n_q):
        band = kv[b * blk:(b + 2) * blk]
        for h in range(KV_HEADS):
            kcol = (h // 2) * PAIR
            kd[b, h] = dup(band[:, kcol:kcol + PAIR], h % 2 == 1)
            vd[b, h] = dup(band[:, kv_width + kcol:kv_width + kcol + PAIR], h % 2 == 1)
    units = [(b, p) for b in range(n_q) for p in range(n_pairs)]

    def qstack(b, p):
        q = q_ref[b * blk:(b + 1) * blk, PAIR * p:PAIR * (p + 1)]
        zero = jnp.zeros_like(q)
        return jnp.concatenate([jnp.where(head0, q, zero), jnp.where(head0, zero, q)], axis=0)

    s = [jnp.where(valid[b], _dot_nt(qstack(b, p), kd[b, (2 * p) // GROUP]), NEG_INF)
         for b, p in units]
    sink = [jnp.where(top, sinks_ref[2 * p], sinks_ref[2 * p + 1]) for _, p in units]
    mx = [jnp.maximum(jnp.max(s_, axis=-1, keepdims=True), k_) for s_, k_ in zip(s, sink)]
    pr = [jnp.exp(s_ - m_) for s_, m_ in zip(s, mx)]
    denom = [jnp.sum(p_, axis=-1, keepdims=True) + jnp.exp(k_ - m_)
             for p_, k_, m_ in zip(pr, sink, mx)]
    o2 = [_dot(pr[u].astype(BF16), vd[b, (2 * p) // GROUP]) / denom[u]
          for u, (b, p) in enumerate(units)]
    for u, (b, p) in enumerate(units):
        o_ref[b * blk:(b + 1) * blk, PAIR * p:PAIR * (p + 1)] = jnp.where(
            head0, o2[u][:blk], o2[u][blk:]).astype(o_ref.dtype)

    out_ref[...] = x_ref[...] + _rms(_dot(o_ref[...], wo_ref[...]), go_ref[...])


def _attention(q, kv, sinks, x2, w_o, g_o, seq):
    m, d = q.shape
    blk = WINDOW
    n_q = 2
    rows = n_q * blk
    row = pl.BlockSpec((rows, d), lambda i: (i, 0))
    return pl.pallas_call(
        functools.partial(_attn_kernel, steps_per_seq=seq // rows),
        grid=(m // rows,),
        in_specs=[
            pl.BlockSpec(memory_space=pltpu.SMEM),
            row,
            pl.BlockSpec((blk, kv.shape[1]), lambda i: (jnp.maximum(i * n_q - 1, 0), 0)),
            pl.BlockSpec((rows, kv.shape[1]), lambda i: (i, 0)),
            row,
            pl.BlockSpec(w_o.shape, lambda i: (0, 0), pipeline_mode=pl.Buffered(1)),
            _const_spec(g_o.shape),
        ],
        out_specs=row,
        out_shape=jax.ShapeDtypeStruct((m, d), F32),
        scratch_shapes=[pltpu.VMEM((rows, d), BF16)],
        compiler_params=pltpu.CompilerParams(
            dimension_semantics=("arbitrary",), vmem_limit_bytes=VMEM_LIMIT),
        name="swa_attention",
    )(sinks, q, kv, kv, x2, w_o, g_o)


def kernel(x, norm_g, mu, w_rkv, w_decay0, w_decay1, w_decay2, a0, a1, a2, g1, g2, k_k, k_a, r_k, gn_g, gn_b, w_o_rwkv, kv_g, w_kv, w_q, sinks, w_o_attn, w_up, conv_w, conv_b, w_down):
    batch, seq, d = x.shape
    n_rwkv = mu.shape[0]
    depth = norm_g.shape[0]
    bf = lambda t: t.astype(BF16)
    vec = lambda t: t.reshape(1, -1)
    x2 = x.reshape(batch * seq, d)
    kv = None
    for l in range(depth):
        if l < n_rwkv:
            i = l
            parts = _rwkv_pre(
                x2, vec(norm_g[l, 0]), mu[i], bf(w_rkv[i, 0]), bf(w_rkv[i, 1]), bf(w_rkv[i, 2]),
                vec(w_decay0[i]), bf(w_decay1[i]), bf(w_decay2[i]),
                vec(a0[i]), bf(a1[i]), bf(a2[i]), bf(g1[i]), bf(g2[i]),
                vec(k_k[i]), vec(k_a[i]), seq)
            x2 = _rwkv_scan(*parts, vec(r_k[i]), vec(gn_g[i]), vec(gn_b[i]),
                            x2, bf(w_o_rwkv[i]), vec(norm_g[l, 1]), batch, seq)
        else:
            j = l - n_rwkv
            if j == 0:
                q, kv = _attn_pre(x2, vec(norm_g[l, 0]), vec(kv_g), bf(w_q[j]), bf(w_kv))
            else:
                q, _ = _attn_pre(x2, vec(norm_g[l, 0]), vec(kv_g), bf(w_q[j]), bf(w_kv))
            x2 = _attention(q, kv, sinks[j], x2, bf(w_o_attn[j]), vec(norm_g[l, 1]), seq)
        x2 = _conv_ffn(x2, vec(norm_g[l, 2]), bf(w_up[l]), conv_w[l], vec(conv_b[l]),
                       bf(w_down[l]), vec(norm_g[l, 3]), seq)
    return x2.reshape(batch, seq, d)
```

```python
import functools
import math

import jax
import jax.numpy as jnp
from jax import lax
from jax.experimental import pallas as pl
from jax.experimental.pallas import tpu as pltpu

F32 = jnp.float32
BF16 = jnp.bfloat16

HEAD_DIM = 64
KV_HEADS = 4
GROUP = 4
WINDOW = 128
RMS_EPS = 1e-6
GN_EPS = 64e-5
NEG_INF = -1e30

LANES = 128
SUBLANES = 8
BF16_ROWS = 16
PAIR = LANES
CHUNK = 64

VMEM_LIMIT = 56 * 1024 * 1024


def _dot(a, b):
    return jnp.dot(a, b, preferred_element_type=F32)


def _dot_nt(a, b):
    return lax.dot_general(a, b, (((1,), (1,)), ((), ())), preferred_element_type=F32)


def _dot_tn(a, b):
    return lax.dot_general(a, b, (((0,), (0,)), ((), ())), preferred_element_type=F32)


def _rms(x, g):
    return x * lax.rsqrt(jnp.mean(x * x, axis=-1, keepdims=True) + RMS_EPS) * g


def _split_bf16(x):
    hi = x.astype(BF16)
    lo = (x - hi.astype(F32)).astype(BF16)
    return hi, lo


def _div_pow2(x, n):
    assert n & (n - 1) == 0
    return x >> (n.bit_length() - 1)


def _block_ones(n, blk, dtype, value=1.0):
    r = _div_pow2(lax.broadcasted_iota(jnp.int32, (n, n), 0), blk)
    c = _div_pow2(lax.broadcasted_iota(jnp.int32, (n, n), 1), blk)
    return jnp.where(r == c, value, 0.0).astype(dtype)


def _const_spec(shape):
    return pl.BlockSpec(shape, lambda *_: (0,) * len(shape))


N_QUARTER = 4


def _rwkv_layer_kernel(x_ref, xn_ref, g_ref, mu_ref, wr_ref, wk_ref, wv_ref,
                       w0_ref, w1_ref, w2_ref, a0_ref, a1_ref, a2_ref, g1_ref, g2_ref,
                       kkw_ref, ka_ref, rk_ref, gng_ref, gnb_ref, wo_ref, go_ref,
                       out_ref,
                       s_ref, y_ref, xr_ref, xk_ref, xv_ref, wlo_ref, alo_ref, glo_ref,
                       *p_refs):
    L = CHUNK
    n_seq, tt, d = xn_ref.shape
    qw = d // N_QUARTER
    n_pairs = d // PAIR
    assert tt == N_QUARTER * L and qw == 2 * PAIR
    bufs = (p_refs[:7], p_refs[7:])
    gin = g_ref[...]

    lane = lax.broadcasted_iota(jnp.int32, (1, PAIR), 1)
    head0 = lane < HEAD_DIM

    def stack2(x):
        return jnp.concatenate([jnp.where(head0, x, 0.0), jnp.where(head0, 0.0, x)], axis=0)

    rr = lax.broadcasted_iota(jnp.int32, (2 * L, 2 * L), 0)
    cc = lax.broadcasted_iota(jnp.int32, (2 * L, 2 * L), 1)
    same_head = _div_pow2(rr, L) == _div_pow2(cc, L)
    eye = rr == cc
    level_masks = []
    s = 4
    while s <= L:
        level_masks.append((_div_pow2(rr, s) == _div_pow2(cc, s))
                           & (_div_pow2(rr, s // 2) != _div_pow2(cc, s // 2)))
        s *= 2
    base_mask = _div_pow2(rr, 2) == _div_pow2(cc, 2)

    tt_i = lax.broadcasted_iota(jnp.int32, (L, 2 * L), 0)
    ss_i = lax.broadcasted_iota(jnp.int32, (L, 2 * L), 1) & (L - 1)
    strict = tt_i > ss_i
    incl = tt_i >= ss_i
    tri = (lax.broadcasted_iota(jnp.int32, (L, L), 0)
           >= lax.broadcasted_iota(jnp.int32, (L, L), 1)).astype(BF16)

    ones2 = _block_ones(PAIR, HEAD_DIM, BF16)
    mean2 = _block_ones(PAIR, HEAD_DIM, BF16, 1.0 / HEAD_DIM)
    ones4 = _block_ones(qw, HEAD_DIM, BF16)


    def pre_front(x_tiles, halos):
        hs, xxs = [], []
        for xt, hp in zip(x_tiles, halos):
            h = _rms(xt, gin)
            hcat = jnp.concatenate([jnp.broadcast_to(hp, (SUBLANES, d)), h], axis=0)
            hs.append(h)
            xxs.append(pltpu.roll(hcat, 1, 0)[SUBLANES:] - h)
        h = jnp.concatenate(hs, axis=0)
        xx = jnp.concatenate(xxs, axis=0)

        def mix(j):
            return (h + xx * mu_ref[j:j + 1, :]).astype(BF16)

        xr_ref[...] = mix(0)
        xk_ref[...] = mix(2)
        xv_ref[...] = mix(3)
        wlo_ref[...] = jnp.tanh(_dot(mix(1), w1_ref[...])).astype(BF16)
        alo_ref[...] = _dot(mix(4), a1_ref[...]).astype(BF16)
        glo_ref[...] = jax.nn.sigmoid(_dot(mix(5), g1_ref[...])).astype(BF16)

    def pre_pieces(q, dst):
        vals = {}

        def piece_r():
            vals["r"] = _dot(xr_ref[...], wr_ref[q])

        def piece_k():
            vals["k"] = _dot(xk_ref[...], wk_ref[q])

        def piece_v():
            vals["v"] = _dot(xv_ref[...], wv_ref[q])

        def piece_rest():
            z = w0_ref[q] + _dot(wlo_ref[...], w2_ref[q])
            a = jax.nn.sigmoid(a0_ref[q] + _dot(alo_ref[...], a2_ref[q]))
            gate = _dot(glo_ref[...], g2_ref[q])
            lw = -jax.nn.sigmoid(z) * math.exp(-0.5)
            k = vals["k"]
            kkr = k * kkw_ref[q]
            hi, lo = _split_bf16(kkr * kkr)
            kk = kkr * lax.rsqrt(jnp.maximum(_dot(hi, ones4) + _dot(lo, ones4), 1e-24))
            outs = (vals["r"], lw, k * (1.0 + (a - 1.0) * ka_ref[q]), vals["v"], kk, kk * a, gate)
            for ref, val in zip(dst, outs):
                for b in range(n_seq):
                    ref[q, b] = val[b * tt:(b + 1) * tt].astype(ref.dtype)

        return [piece_r, piece_k, piece_v, piece_rest]


    units = [(b, p) for b in range(n_seq) for p in range(n_pairs)]
    pairs = range(len(units))
    sq = [b for b, _ in units]
    quarter = [p // 2 for _, p in units]
    qcol = [slice(PAIR * (p % 2), PAIR * (p % 2 + 1)) for _, p in units]
    col = [slice(PAIR * p, PAIR * (p + 1)) for _, p in units]

    def scan_tile(src, dst):
        r_ref, lw_ref, k_ref, v_ref, kk_ref, bb_ref, gate_ref = src

        def chunk_body(ci, carry):
            pieces = pre_pieces(ci, dst)
            rows = pl.ds(pl.multiple_of(ci * L, L), L)
            ld = lambda ref: [ref[quarter[p], sq[p], rows, qcol[p]].astype(F32) for p in pairs]

            lw = ld(lw_ref)
            split = [_split_bf16(x) for x in lw]
            cum = [_dot(tri, hi) + _dot(tri, lo) for hi, lo in split]
            total = [c[L - 1:L, :] for c in cum]
            e_neg = [jnp.exp(-c) for c in cum]
            rt = [x * jnp.exp(c) for x, c in zip(ld(r_ref), cum)]
            kt = [x * e for x, e in zip(ld(k_ref), e_neg)]
            at = [-x * jnp.exp(c - w) for x, c, w in zip(ld(kk_ref), cum, lw)]
            bt = [x * e for x, e in zip(ld(bb_ref), e_neg)]

            gmat = [_dot_nt(jnp.concatenate([at[p], rt[p]], axis=0).astype(BF16),
                            jnp.concatenate([stack2(bt[p]), stack2(kt[p])], axis=0).astype(BF16))
                    for p in pairs]
            pieces[0]()
            h_ak = [jnp.where(strict, g_[:L, 2 * L:], 0.0) for g_ in gmat]
            h_rb = [jnp.where(incl, g_[L:, :2 * L], 0.0).astype(BF16) for g_ in gmat]
            h_rk = [jnp.where(incl, g_[L:, 2 * L:], 0.0) for g_ in gmat]

            nbd = [stack2(jnp.where(strict, g_[:L, :2 * L], 0.0)) for g_ in gmat]
            tinv = [jnp.where(eye, 1.0, jnp.where(base_mask, n_, 0.0)) for n_ in nbd]
            for lvl, lm in enumerate(level_masks):
                tb = [t_.astype(BF16) for t_ in tinv]
                x = [_dot(tb[p], jnp.where(lm, nbd[p], 0.0).astype(BF16)) for p in pairs]
                tinv = [tinv[p] + _dot(x[p].astype(BF16), tb[p]) for p in pairs]
                if lvl == 1:
                    pieces[1]()
                if lvl == 3:
                    pieces[2]()
            th = [(t_[:L] + t_[L:]).astype(BF16) for t_ in tinv]

            v = ld(v_ref)
            zy = [_dot(jnp.concatenate([h_ak[p], h_rk[p]], axis=0).astype(BF16),
                       stack2(v[p]).astype(BF16)) for p in pairs]
            aw = [_dot(th[p], jnp.concatenate([stack2(at[p]), stack2(zy[p][:L])], axis=1).astype(BF16))
                  for p in pairs]

            state = [s_ref[p] for p in pairs]
            uy = [_dot_nt(jnp.concatenate([aw[p][:, :PAIR], rt[p]], axis=0).astype(BF16),
                          state[p].astype(BF16)) for p in pairs]
            u = [uy[p][:L] + aw[p][:, PAIR:] for p in pairs]
            y = [uy[p][L:] + zy[p][L:] + _dot(h_rb[p], stack2(u[p]).astype(BF16)) for p in pairs]

            k = ld(k_ref)
            bb = ld(bb_ref)
            e_end = [jnp.exp(total[p] - cum[p]) for p in pairs]
            upd = [_dot_tn(jnp.concatenate([u[p], v[p]], axis=0).astype(BF16),
                           jnp.concatenate([bb[p] * e_end[p], k[p] * e_end[p]], axis=0).astype(BF16))
                   for p in pairs]
            for p in pairs:
                s_ref[p] = state[p] * jnp.exp(total[p]) + jnp.where(same_head, upd[p], 0.0)
            pieces[3]()

            ysplit = [_split_bf16(y_) for y_ in y]
            yc = [y[p] - (_dot(ysplit[p][0], mean2) + _dot(ysplit[p][1], mean2)) for p in pairs]
            var = [_dot((c * c).astype(BF16), mean2) for c in yc]
            r = ld(r_ref)
            bonus = [_dot((r[p] * k[p] * rk_ref[:, col[p]]).astype(BF16), ones2) * v[p]
                     for p in pairs]
            gate = ld(gate_ref)
            for p in pairs:
                yn = yc[p] * lax.rsqrt(var[p] + GN_EPS) * gng_ref[:, col[p]] + gnb_ref[:, col[p]]
                y_ref[sq[p], rows, col[p]] = ((yn + bonus[p]) * gate[p]).astype(y_ref.dtype)
            return carry

        lax.fori_loop(0, N_QUARTER, chunk_body, 0)

    def finish_tile(r0):
        for b in range(n_seq):
            out_ref[b, pl.ds(r0, tt), :] = x_ref[b, pl.ds(r0, tt), :] + _rms(
                _dot(y_ref[b], wo_ref[...]), go_ref[...])

    def last_row(r):
        return [_rms(x_ref[b, pl.ds(r, 1), :], gin) for b in range(n_seq)]

    @pl.when(pl.program_id(1) == 0)
    def _():
        s_ref[...] = jnp.zeros_like(s_ref)
        pre_front([x_ref[b, pl.ds(0, tt), :] for b in range(n_seq)],
                  [jnp.zeros((1, d), F32)] * n_seq)
        for q in range(N_QUARTER):
            for piece in pre_pieces(q, bufs[0]):
                piece()

    pre_front([x_ref[b, pl.ds(tt, tt), :] for b in range(n_seq)], last_row(tt - 1))
    scan_tile(bufs[0], bufs[1])
    finish_tile(0)
    pre_front([xn_ref[b] for b in range(n_seq)], last_row(2 * tt - 1))
    scan_tile(bufs[1], bufs[0])
    finish_tile(tt)


def _rwkv_layer(x2, norm_g, mu, wr, wk, wv, w0, w1, w2, a0, a1, a2, g1, g2, k_k, k_a,
                r_k, gn_g, gn_b, w_o, g_o, batch, seq):
    m, d = x2.shape
    tt = N_QUARTER * CHUNK
    n_seq = 2
    steps = seq // (2 * tt)
    tiles = seq // tt
    qw = d // N_QUARTER

    def quarters(t):
        return t.reshape(t.shape[0], N_QUARTER, qw).swapaxes(0, 1)

    x4 = x2.reshape(batch // n_seq, n_seq, seq, d)
    two_tiles = pl.BlockSpec((None, n_seq, 2 * tt, d), lambda b, c: (b, 0, c, 0))
    two_tiles_in = pl.BlockSpec((None, n_seq, 2 * tt, d), lambda b, c: (b, 0, c, 0),
                                pipeline_mode=pl.Buffered(1))
    next_tile = pl.BlockSpec((None, n_seq, tt, d),
                             lambda b, c: (b, 0, jnp.minimum(2 * c + 2, tiles - 1), 0),
                             pipeline_mode=pl.Buffered(1))
    ins = [norm_g, mu, quarters(wr), quarters(wk), quarters(wv),
           quarters(w0), w1, quarters(w2), quarters(a0), a1, quarters(a2), g1, quarters(g2),
           quarters(k_k), quarters(k_a), r_k, gn_g, gn_b, w_o, g_o]
    single = pl.Buffered(1)
    in_specs = [two_tiles_in, next_tile] + [
        pl.BlockSpec(t.shape, functools.partial(lambda n, b, c: (0,) * n, t.ndim), pipeline_mode=single)
        for t in ins]
    p_shape = (N_QUARTER, n_seq, tt, qw)
    p_buf = [pltpu.VMEM(p_shape, F32 if j == 1 else BF16) for j in range(7)]
    rows = n_seq * tt
    out = pl.pallas_call(
        _rwkv_layer_kernel,
        grid=(batch // n_seq, steps),
        in_specs=in_specs,
        out_specs=two_tiles,
        out_shape=jax.ShapeDtypeStruct(x4.shape, F32),
        scratch_shapes=[pltpu.VMEM((n_seq * (d // PAIR), PAIR, PAIR), F32),
                        pltpu.VMEM((n_seq, tt, d), BF16),
                        pltpu.VMEM((rows, d), BF16), pltpu.VMEM((rows, d), BF16),
                        pltpu.VMEM((rows, d), BF16),
                        pltpu.VMEM((rows, w1.shape[1]), BF16), pltpu.VMEM((rows, a1.shape[1]), BF16),
                        pltpu.VMEM((rows, g1.shape[1]), BF16)] + p_buf + p_buf,
        compiler_params=pltpu.CompilerParams(
            dimension_semantics=("arbitrary", "arbitrary"), vmem_limit_bytes=60 * 1024 * 1024),
        name="rwkv_layer",
    )(x4, x4, *ins)
    return out.reshape(m, d)


def _ffn_kernel(x_ref, xh_ref, gin_ref, wup_ref, cw_ref, cb_ref, wdn_ref, gout_ref,
                out_ref, xcat_ref, h_ref, act_ref, acc_ref, *, tiles_per_seq, d_ff, f_chunk):
    i = pl.program_id(0)
    halo = xh_ref.shape[0]
    tm = x_ref.shape[0]
    gin = gin_ref[...]
    xp = _rms(xh_ref[...], gin)
    xcat_ref[:halo, :] = jnp.where(i % tiles_per_seq == 0, 0.0, xp).astype(BF16)
    xcat_ref[halo:, :] = _rms(x_ref[...], gin).astype(BF16)
    c_gelu = math.sqrt(2.0 / math.pi)
    n_chunks = d_ff // f_chunk

    n_units = 2
    rows_u = tm // n_units

    def up(j, q):
        r0 = 0 if q == 0 else halo + q * rows_u
        rows = pl.ds(r0, halo + (q + 1) * rows_u - r0)
        for half in range(2):
            c0 = half * d_ff + j * f_chunk
            h_ref[j % 2, half, rows, :] = _dot(xcat_ref[rows, :], wup_ref[:, c0:c0 + f_chunk])

    def act(j, q):
        def conv(half):
            col = slice(half * d_ff + j * f_chunk, half * d_ff + (j + 1) * f_chunk)
            w = cw_ref[:, col]
            r0 = halo + q * rows_u
            return (h_ref[j % 2, half, pl.ds(r0, rows_u), :] * w[2:3]
                    + h_ref[j % 2, half, pl.ds(r0 - 1, rows_u), :] * w[1:2]
                    + h_ref[j % 2, half, pl.ds(r0 - 2, rows_u), :] * w[0:1] + cb_ref[:, col])
        a = conv(0)
        gelu = a * (0.5 * (1.0 + jnp.tanh(c_gelu * (a + 0.044715 * (a * a * a)))))
        act_ref[j % 2, pl.ds(q * rows_u, rows_u), :] = (gelu * conv(1)).astype(BF16)

    def down(j, q):
        rows = pl.ds(q * rows_u, rows_u)
        part = _dot(act_ref[j % 2, rows, :], wdn_ref[j * f_chunk:(j + 1) * f_chunk, :])
        if j == 0:
            acc_ref[rows, :] = part
        else:
            acc_ref[rows, :] += part

    for j in range(n_chunks):
        for q in range(n_units):
            up(j, q)
        for q in range(n_units):
            act(j, q)
        for q in range(n_units):
            down(j, q)
    out_ref[...] = x_ref[...] + _rms(acc_ref[...], gout_ref[...])


def _conv_ffn(x2, g_in, w_up, conv_w, conv_b, w_down, g_out, seq):
    m, d = x2.shape
    d_ff = w_down.shape[0]
    tm = 512
    f_chunk = 256
    row = pl.BlockSpec((tm, d), lambda i: (i, 0))
    halo = pl.BlockSpec((BF16_ROWS, d), lambda i: (jnp.maximum(i * (tm // BF16_ROWS) - 1, 0), 0))
    single = pl.Buffered(1)
    in_specs = [
        row, halo, _const_spec(g_in.shape),
        pl.BlockSpec(w_up.shape, lambda i: (0, 0), pipeline_mode=single),
        _const_spec(conv_w.shape), _const_spec(conv_b.shape),
        pl.BlockSpec(w_down.shape, lambda i: (0, 0), pipeline_mode=single),
        _const_spec(g_out.shape),
    ]
    return pl.pallas_call(
        functools.partial(_ffn_kernel, tiles_per_seq=seq // tm, d_ff=d_ff, f_chunk=f_chunk),
        grid=(m // tm,),
        in_specs=in_specs,
        out_specs=row,
        out_shape=jax.ShapeDtypeStruct((m, d), F32),
        scratch_shapes=[pltpu.VMEM((BF16_ROWS + tm, d), BF16),
                        pltpu.VMEM((2, 2, BF16_ROWS + tm, f_chunk), F32),
                        pltpu.VMEM((2, tm, f_chunk), BF16),
                        pltpu.VMEM((tm, d), F32)],
        compiler_params=pltpu.CompilerParams(
            dimension_semantics=("arbitrary",), vmem_limit_bytes=VMEM_LIMIT),
        name="conv_ffn",
    )(x2, x2, g_in, w_up, conv_w, conv_b, w_down, g_out)


def _attn_pre_kernel(x_ref, gq_ref, gkv_ref, wq_ref, wkv_ref, q_out, kv_out):
    x = x_ref[...]
    xn = x * lax.rsqrt(jnp.mean(x * x, axis=-1, keepdims=True) + RMS_EPS)
    q = _dot((xn * gq_ref[...]).astype(BF16), wq_ref[...])
    q_out[...] = (q * (HEAD_DIM ** -0.5)).astype(q_out.dtype)
    kv_out[...] = _dot((xn * gkv_ref[...]).astype(BF16), wkv_ref[...]).astype(kv_out.dtype)


def _attn_pre(x2, g_q, g_kv, w_q, w_kv):
    m, d = x2.shape
    tm = 512
    row = pl.BlockSpec((tm, d), lambda i: (i, 0))
    return pl.pallas_call(
        _attn_pre_kernel,
        grid=(m // tm,),
        in_specs=[row, _const_spec(g_q.shape), _const_spec(g_kv.shape),
                  _const_spec(w_q.shape), _const_spec(w_kv.shape)],
        out_specs=[pl.BlockSpec((tm, w_q.shape[1]), lambda i: (i, 0)),
                   pl.BlockSpec((tm, w_kv.shape[1]), lambda i: (i, 0))],
        out_shape=[jax.ShapeDtypeStruct((m, w_q.shape[1]), BF16),
                   jax.ShapeDtypeStruct((m, w_kv.shape[1]), BF16)],
        compiler_params=pltpu.CompilerParams(
            dimension_semantics=("arbitrary",), vmem_limit_bytes=VMEM_LIMIT),
        name="attn_pre",
    )(x2, g_q, g_kv, w_q, w_kv)


def _attn_kernel(sinks_ref, q_ref, kvp_ref, kvo_ref, x_ref, wo_ref, go_ref,
                 out_ref, o_ref, *, steps_per_seq):
    blk = kvp_ref.shape[0]
    n_q = q_ref.shape[0] // blk
    n_pairs = q_ref.shape[1] // PAIR
    kv_width = KV_HEADS * HEAD_DIM
    first = (pl.program_id(0) % steps_per_seq) == 0

    lane = lax.broadcasted_iota(jnp.int32, (1, PAIR), 1)
    head0 = lane < HEAD_DIM
    row = lax.broadcasted_iota(jnp.int32, (2 * blk, 2 * blk), 0)
    col = lax.broadcasted_iota(jnp.int32, (2 * blk, 2 * blk), 1)
    rel = (row & (blk - 1)) - (col - blk)
    in_window = (rel >= 0) & (rel < WINDOW)
    valid = [in_window & (jnp.logical_not(first) | (col >= blk))] + [in_window] * (n_q - 1)
    top = lax.broadcasted_iota(jnp.int32, (2 * blk, 1), 0) < blk

    kv = jnp.concatenate([kvp_ref[...], kvo_ref[...]], axis=0)

    def dup(x, odd):
        rolled = pltpu.roll(x, HEAD_DIM, 1)
        return jnp.where(head0, rolled, x) if odd else jnp.where(head0, x, rolled)

    kd, vd = {}, {}
    for b in range(n_q):
        band = kv[b * blk:(b + 2) * blk]
        for h in range(KV_HEADS):
            kcol = (h // 2) * PAIR
            kd[b, h] = dup(band[:, kcol:kcol + PAIR], h % 2 == 1)
            vd[b, h] = dup(band[:, kv_width + kcol:kv_width + kcol + PAIR], h % 2 == 1)
    units = [(b, p) for b in range(n_q) for p in range(n_pairs)]

    def qstack(b, p):
        q = q_ref[b * blk:(b + 1) * blk, PAIR * p:PAIR * (p + 1)]
        zero = jnp.zeros_like(q)
        return jnp.concatenate([jnp.where(head0, q, zero), jnp.where(head0, zero, q)], axis=0)

    s = [jnp.where(valid[b], _dot_nt(qstack(b, p), kd[b, (2 * p) // GROUP]), NEG_INF)
         for b, p in units]
    sink = [jnp.where(top, sinks_ref[2 * p], sinks_ref[2 * p + 1]) for _, p in units]
    mx = [jnp.maximum(jnp.max(s_, axis=-1, keepdims=True), k_) for s_, k_ in zip(s, sink)]
    pr = [jnp.exp(s_ - m_) for s_, m_ in zip(s, mx)]
    denom = [jnp.sum(p_, axis=-1, keepdims=True) + jnp.exp(k_ - m_)
             for p_, k_, m_ in zip(pr, sink, mx)]
    o2 = [_dot(pr[u].astype(BF16), vd[b, (2 * p) // GROUP]) / denom[u]
          for u, (b, p) in enumerate(units)]
    for u, (b, p) in enumerate(units):
        o_ref[b * blk:(b + 1) * blk, PAIR * p:PAIR * (p + 1)] = jnp.where(
            head0, o2[u][:blk], o2[u][blk:]).astype(o_ref.dtype)

    out_ref[...] = x_ref[...] + _rms(_dot(o_ref[...], wo_ref[...]), go_ref[...])


def _attention(q, kv, sinks, x2, w_o, g_o, seq):
    m, d = q.shape
    blk = WINDOW
    n_q = 2
    rows = n_q * blk
    row = pl.BlockSpec((rows, d), lambda i: (i, 0))
    return pl.pallas_call(
        functools.partial(_attn_kernel, steps_per_seq=seq // rows),
        grid=(m // rows,),
        in_specs=[
            pl.BlockSpec(memory_space=pltpu.SMEM),
            row,
            pl.BlockSpec((blk, kv.shape[1]), lambda i: (jnp.maximum(i * n_q - 1, 0), 0)),
            pl.BlockSpec((rows, kv.shape[1]), lambda i: (i, 0)),
            row,
            pl.BlockSpec(w_o.shape, lambda i: (0, 0), pipeline_mode=pl.Buffered(1)),
            _const_spec(g_o.shape),
        ],
        out_specs=row,
        out_shape=jax.ShapeDtypeStruct((m, d), F32),
        scratch_shapes=[pltpu.VMEM((rows, d), BF16)],
        compiler_params=pltpu.CompilerParams(
            dimension_semantics=("arbitrary",), vmem_limit_bytes=VMEM_LIMIT),
        name="swa_attention",
    )(sinks, q, kv, kv, x2, w_o, g_o)


def kernel(x, norm_g, mu, w_rkv, w_decay0, w_decay1, w_decay2, a0, a1, a2, g1, g2, k_k, k_a, r_k, gn_g, gn_b, w_o_rwkv, kv_g, w_kv, w_q, sinks, w_o_attn, w_up, conv_w, conv_b, w_down):
    batch, seq, d = x.shape
    n_rwkv = mu.shape[0]
    depth = norm_g.shape[0]
    bf = lambda t: t.astype(BF16)
    vec = lambda t: t.reshape(1, -1)
    x2 = x.reshape(batch * seq, d)
    kv = None
    for l in range(depth):
        if l < n_rwkv:
            i = l
            x2 = _rwkv_layer(
                x2, vec(norm_g[l, 0]), mu[i], bf(w_rkv[i, 0]), bf(w_rkv[i, 1]), bf(w_rkv[i, 2]),
                vec(w_decay0[i]), bf(w_decay1[i]), bf(w_decay2[i]),
                vec(a0[i]), bf(a1[i]), bf(a2[i]), bf(g1[i]), bf(g2[i]),
                vec(k_k[i]), vec(k_a[i]), vec(r_k[i]), vec(gn_g[i]), vec(gn_b[i]),
                bf(w_o_rwkv[i]), vec(norm_g[l, 1]), batch, seq)
        else:
            j = l - n_rwkv
            if j == 0:
                q, kv = _attn_pre(x2, vec(norm_g[l, 0]), vec(kv_g), bf(w_q[j]), bf(w_kv))
            else:
                q, _ = _attn_pre(x2, vec(norm_g[l, 0]), vec(kv_g), bf(w_q[j]), bf(w_kv))
            x2 = _attention(q, kv, sinks[j], x2, bf(w_o_attn[j]), vec(norm_g[l, 1]), seq)
        x2 = _conv_ffn(x2, vec(norm_g[l, 2]), bf(w_up[l]), conv_w[l], vec(conv_b[l]),
                       bf(w_down[l]), vec(norm_g[l, 3]), seq)
    return x2.reshape(batch, seq, d)
```

```python
import functools
import math

import jax
import jax.numpy as jnp
from jax import lax
from jax.experimental import pallas as pl
from jax.experimental.pallas import tpu as pltpu

F32 = jnp.float32
BF16 = jnp.bfloat16

HEAD_DIM = 64
KV_HEADS = 4
GROUP = 4
WINDOW = 128
RMS_EPS = 1e-6
GN_EPS = 64e-5
NEG_INF = -1e30

LANES = 128
SUBLANES = 8
BF16_ROWS = 16
PAIR = LANES
CHUNK = 64

VMEM_LIMIT = 56 * 1024 * 1024


def _dot(a, b):
    return jnp.dot(a, b, preferred_element_type=F32)


def _dot_nt(a, b):
    return lax.dot_general(a, b, (((1,), (1,)), ((), ())), preferred_element_type=F32)


def _dot_tn(a, b):
    return lax.dot_general(a, b, (((0,), (0,)), ((), ())), preferred_element_type=F32)


def _rms(x, g):
    return x * lax.rsqrt(jnp.mean(x * x, axis=-1, keepdims=True) + RMS_EPS) * g


def _split_bf16(x):
    hi = x.astype(BF16)
    lo = (x - hi.astype(F32)).astype(BF16)
    return hi, lo


def _div_pow2(x, n):
    assert n & (n - 1) == 0
    return x >> (n.bit_length() - 1)


def _block_ones(n, blk, dtype, value=1.0):
    r = _div_pow2(lax.broadcasted_iota(jnp.int32, (n, n), 0), blk)
    c = _div_pow2(lax.broadcasted_iota(jnp.int32, (n, n), 1), blk)
    return jnp.where(r == c, value, 0.0).astype(dtype)


def _const_spec(shape):
    return pl.BlockSpec(shape, lambda *_: (0,) * len(shape))


def _rwkv_pre_kernel(x_ref, xh_ref, g_ref, mu_ref, wr_ref, wk_ref, wv_ref,
                     w0_ref, w1_ref, w2_ref, a0_ref, a1_ref, a2_ref,
                     g1_ref, g2_ref, kk_ref, ka_ref,
                     r_out, lw_out, k_out, v_out, kk_out, bb_out, g_out,
                     *, tiles_per_seq):
    i = pl.program_id(0)
    g = g_ref[...]
    h = _rms(x_ref[...], g)
    hp = _rms(xh_ref[...], g)
    hp = jnp.where(i % tiles_per_seq == 0, 0.0, hp)
    hcat = jnp.concatenate([hp, h], axis=0)
    prev = pltpu.roll(hcat, 1, 0)[SUBLANES:]
    xx = prev - h

    def mix(j):
        return (h + xx * mu_ref[j:j + 1, :]).astype(BF16)

    w_lo = _dot(mix(1), w1_ref[...])
    a_lo = _dot(mix(4), a1_ref[...])
    g_lo = _dot(mix(5), g1_ref[...])
    r = _dot(mix(0), wr_ref[...])
    k = _dot(mix(2), wk_ref[...])
    v = _dot(mix(3), wv_ref[...])
    z = w0_ref[...] + _dot(jnp.tanh(w_lo).astype(BF16), w2_ref[...])
    a = jax.nn.sigmoid(a0_ref[...] + _dot(a_lo.astype(BF16), a2_ref[...]))
    gate = _dot(jax.nn.sigmoid(g_lo).astype(BF16), g2_ref[...])

    lw = -jax.nn.sigmoid(z) * math.exp(-0.5)

    kkr = k * kk_ref[...]
    sq = kkr * kkr
    d = sq.shape[-1]
    ones4 = _block_ones(2 * PAIR, HEAD_DIM, BF16)
    parts = []
    for p in range(d // (2 * PAIR)):
        hi, lo = _split_bf16(sq[:, 2 * PAIR * p:2 * PAIR * (p + 1)])
        parts.append(_dot(hi, ones4) + _dot(lo, ones4))
    ss = jnp.concatenate(parts, axis=1)
    kk = kkr * lax.rsqrt(jnp.maximum(ss, 1e-24))

    r_out[...] = r.astype(r_out.dtype)
    lw_out[...] = lw
    k_out[...] = (k * (1.0 + (a - 1.0) * ka_ref[...])).astype(k_out.dtype)
    v_out[...] = v.astype(v_out.dtype)
    kk_out[...] = kk.astype(kk_out.dtype)
    bb_out[...] = (kk * a).astype(bb_out.dtype)
    g_out[...] = gate.astype(g_out.dtype)


def _rwkv_pre(x2, norm_g, mu, w_rkv, layer, w0, w1, w2, a0, a1, a2, g1, g2, k_k, k_a, seq):
    m, d = x2.shape
    tm = 256
    row = pl.BlockSpec((tm, d), lambda i: (i, 0))
    halo = pl.BlockSpec((SUBLANES, d), lambda i: (jnp.maximum(i * (tm // SUBLANES) - 1, 0), 0))
    ins = [x2, x2, norm_g, mu, w_rkv, w_rkv, w_rkv, w0, w1, w2, a0, a1, a2, g1, g2, k_k, k_a]
    rkv_specs = [pl.BlockSpec((None, None, d, d), functools.partial(lambda j, i: (layer, j, 0, 0), j))
                 for j in range(3)]
    in_specs = ([row, halo] + [_const_spec(t.shape) for t in ins[2:4]] + rkv_specs
                + [_const_spec(t.shape) for t in ins[7:]])
    out = [jax.ShapeDtypeStruct((m, d), F32 if j == 1 else BF16) for j in range(7)]
    return pl.pallas_call(
        functools.partial(_rwkv_pre_kernel, tiles_per_seq=seq // tm),
        grid=(m // tm,),
        in_specs=in_specs,
        out_specs=[row] * 7,
        out_shape=out,
        compiler_params=pltpu.CompilerParams(
            dimension_semantics=("arbitrary",), vmem_limit_bytes=VMEM_LIMIT),
        name="rwkv_pre",
    )(*ins)


def _scan_kernel(r_ref, lw_ref, k_ref, v_ref, kk_ref, bb_ref, g_ref,
                 rk_ref, gng_ref, gnb_ref, x_ref, wo_ref, go_ref,
                 out_ref, s_ref, y_ref, *, n_chunks):
    L = CHUNK
    n_seq = r_ref.shape[0]
    n_pairs = r_ref.shape[2] // PAIR

    @pl.when(pl.program_id(1) == 0)
    def _():
        s_ref[...] = jnp.zeros_like(s_ref)

    lane = lax.broadcasted_iota(jnp.int32, (1, PAIR), 1)
    head0 = lane < HEAD_DIM

    def stack2(x):
        return jnp.concatenate([jnp.where(head0, x, 0.0), jnp.where(head0, 0.0, x)], axis=0)

    rr = lax.broadcasted_iota(jnp.int32, (2 * L, 2 * L), 0)
    cc = lax.broadcasted_iota(jnp.int32, (2 * L, 2 * L), 1)
    same_head = _div_pow2(rr, L) == _div_pow2(cc, L)
    eye = rr == cc
    level_masks = []
    s = 4
    while s <= L:
        level_masks.append((_div_pow2(rr, s) == _div_pow2(cc, s))
                           & (_div_pow2(rr, s // 2) != _div_pow2(cc, s // 2)))
        s *= 2
    base_mask = _div_pow2(rr, 2) == _div_pow2(cc, 2)

    tt = lax.broadcasted_iota(jnp.int32, (L, 2 * L), 0)
    ss_ = lax.broadcasted_iota(jnp.int32, (L, 2 * L), 1) & (L - 1)
    strict = tt > ss_
    incl = tt >= ss_
    tri = (lax.broadcasted_iota(jnp.int32, (L, L), 0)
           >= lax.broadcasted_iota(jnp.int32, (L, L), 1)).astype(BF16)

    ones2 = _block_ones(PAIR, HEAD_DIM, BF16)
    mean2 = _block_ones(PAIR, HEAD_DIM, BF16, 1.0 / HEAD_DIM)

    units = [(b, p) for b in range(n_seq) for p in range(n_pairs)]
    pairs = range(len(units))
    sq = [b for b, _ in units]
    col = [slice(PAIR * p, PAIR * (p + 1)) for _, p in units]

    def chunk_body(ci, carry):
        rows = pl.ds(pl.multiple_of(ci * L, L), L)
        ld = lambda ref: [ref[sq[p], rows, col[p]].astype(F32) for p in pairs]

        lw = ld(lw_ref)
        cum2 = [_dot(tri, jnp.concatenate(_split_bf16(x), axis=1)) for x in lw]
        cum = [c[:, :PAIR] + c[:, PAIR:] for c in cum2]
        total = [c[L - 1:L, :] for c in cum]
        e_neg = [jnp.exp(-c) for c in cum]
        rt = [x * jnp.exp(c) for x, c in zip(ld(r_ref), cum)]
        kt = [x * e for x, e in zip(ld(k_ref), e_neg)]
        at = [-x * jnp.exp(c - w) for x, c, w in zip(ld(kk_ref), cum, lw)]
        bt = [x * e for x, e in zip(ld(bb_ref), e_neg)]

        gmat = [_dot_nt(jnp.concatenate([at[p], rt[p]], axis=0).astype(BF16),
                        jnp.concatenate([stack2(bt[p]), stack2(kt[p])], axis=0).astype(BF16))
                for p in pairs]
        h_ak = [jnp.where(strict, g_[:L, 2 * L:], 0.0) for g_ in gmat]
        h_rb = [jnp.where(incl, g_[L:, :2 * L], 0.0).astype(BF16) for g_ in gmat]
        h_rk = [jnp.where(incl, g_[L:, 2 * L:], 0.0) for g_ in gmat]

        nbd = [stack2(jnp.where(strict, g_[:L, :2 * L], 0.0)) for g_ in gmat]
        tinv = [jnp.where(eye, 1.0, jnp.where(base_mask, n_, 0.0)) for n_ in nbd]
        for lm in level_masks:
            tb = [t_.astype(BF16) for t_ in tinv]
            x = [_dot(tb[p], jnp.where(lm, nbd[p], 0.0).astype(BF16)) for p in pairs]
            tinv = [tinv[p] + _dot(x[p].astype(BF16), tb[p]) for p in pairs]
        th = [(t_[:L] + t_[L:]).astype(BF16) for t_ in tinv]

        v = ld(v_ref)
        zy = [_dot(jnp.concatenate([h_ak[p], h_rk[p]], axis=0).astype(BF16),
                   stack2(v[p]).astype(BF16)) for p in pairs]
        aw = [_dot(th[p], jnp.concatenate([stack2(at[p]), stack2(zy[p][:L])], axis=1).astype(BF16))
              for p in pairs]

        state = [s_ref[p] for p in pairs]
        uy = [_dot_nt(jnp.concatenate([aw[p][:, :PAIR], rt[p]], axis=0).astype(BF16),
                      state[p].astype(BF16)) for p in pairs]
        u = [uy[p][:L] + aw[p][:, PAIR:] for p in pairs]
        y = [uy[p][L:] + zy[p][L:] + _dot(h_rb[p], stack2(u[p]).astype(BF16)) for p in pairs]

        k = ld(k_ref)
        bb = ld(bb_ref)
        e_end = [jnp.exp(total[p] - cum[p]) for p in pairs]
        upd = [_dot_tn(jnp.concatenate([u[p], v[p]], axis=0).astype(BF16),
                       jnp.concatenate([bb[p] * e_end[p], k[p] * e_end[p]], axis=0).astype(BF16))
               for p in pairs]
        for p in pairs:
            s_ref[p] = state[p] * jnp.exp(total[p]) + jnp.where(same_head, upd[p], 0.0)

        yc = [y_ - _dot(y_.astype(BF16), mean2) for y_ in y]
        var = [_dot((c * c).astype(BF16), mean2) for c in yc]
        r = ld(r_ref)
        bonus = [_dot((r[p] * k[p] * rk_ref[:, col[p]]).astype(BF16), ones2) * v[p] for p in pairs]
        for p in pairs:
            yn = yc[p] * lax.rsqrt(var[p] + GN_EPS) * gng_ref[:, col[p]] + gnb_ref[:, col[p]]
            y_ref[sq[p], rows, col[p]] = ((yn + bonus[p]) * g_ref[sq[p], rows, col[p]]).astype(y_ref.dtype)
        return carry

    lax.fori_loop(0, n_chunks, chunk_body, 0)

    for b in range(n_seq):
        out_ref[b] = x_ref[b] + _rms(_dot(y_ref[b], wo_ref[...]), go_ref[...])


def _rwkv_scan(r, lw, k, v, kk, bb, g, r_k, gn_g, gn_b, x2, w_o, g_o, batch, seq):
    m, d = r.shape
    tt = 256
    n_seq = 2
    steps = seq // tt
    as4d = lambda t: t.reshape(batch // n_seq, n_seq, seq, d)
    row = pl.BlockSpec((None, n_seq, tt, d), lambda b, c: (b, 0, c, 0))
    vec = pl.BlockSpec((1, d), lambda b, c: (0, 0))
    out = pl.pallas_call(
        functools.partial(_scan_kernel, n_chunks=tt // CHUNK),
        grid=(batch // n_seq, steps),
        in_specs=[row] * 7 + [vec] * 3
        + [row, pl.BlockSpec(w_o.shape, lambda b, c: (0, 0), pipeline_mode=pl.Buffered(1)), vec],
        out_specs=row,
        out_shape=jax.ShapeDtypeStruct((batch // n_seq, n_seq, seq, d), F32),
        scratch_shapes=[pltpu.VMEM((n_seq * (d // PAIR), PAIR, PAIR), F32),
                        pltpu.VMEM((n_seq, tt, d), BF16)],
        compiler_params=pltpu.CompilerParams(
            dimension_semantics=("arbitrary", "arbitrary"), vmem_limit_bytes=VMEM_LIMIT),
        name="rwkv_scan",
    )(*[as4d(t) for t in (r, lw, k, v, kk, bb, g)], r_k, gn_g, gn_b, as4d(x2), w_o, g_o)
    return out.reshape(m, d)


def _ffn_kernel(x_ref, xh_ref, gin_ref, wup_ref, cw_ref, cb_ref, wdn_ref, gout_ref,
                out_ref, xcat_ref, h_ref, act_ref, acc_ref, *, tiles_per_seq, d_ff, f_chunk):
    i = pl.program_id(0)
    halo = xh_ref.shape[0]
    tm = x_ref.shape[0]
    gin = gin_ref[...]
    xp = _rms(xh_ref[...], gin)
    xcat_ref[:halo, :] = jnp.where(i % tiles_per_seq == 0, 0.0, xp).astype(BF16)
    xcat_ref[halo:, :] = _rms(x_ref[...], gin).astype(BF16)
    c_gelu = math.sqrt(2.0 / math.pi)
    n_chunks = d_ff // f_chunk

    def up(j):
        for half in range(2):
            c0 = half * d_ff + j * f_chunk
            h_ref[j % 2, half] = _dot(xcat_ref[...], wup_ref[:, c0:c0 + f_chunk])

    def conv(j, half):
        col = slice(half * d_ff + j * f_chunk, half * d_ff + (j + 1) * f_chunk)
        w = cw_ref[:, col]
        return (h_ref[j % 2, half, pl.ds(halo, tm), :] * w[2:3]
                + h_ref[j % 2, half, pl.ds(halo - 1, tm), :] * w[1:2]
                + h_ref[j % 2, half, pl.ds(halo - 2, tm), :] * w[0:1] + cb_ref[:, col])

    def down(j):
        part = _dot(act_ref[j % 2], wdn_ref[j * f_chunk:(j + 1) * f_chunk, :])
        if j == 0:
            acc_ref[...] = part
        else:
            acc_ref[...] += part

    up(0)
    for j in range(n_chunks):
        if j + 1 < n_chunks:
            up(j + 1)
        if j > 0:
            down(j - 1)
        a = conv(j, 0)
        u = conv(j, 1)
        gelu = a * (0.5 * (1.0 + jnp.tanh(c_gelu * (a + 0.044715 * (a * a * a)))))
        act_ref[j % 2] = (gelu * u).astype(BF16)
    down(n_chunks - 1)
    out_ref[...] = x_ref[...] + _rms(acc_ref[...], gout_ref[...])


def _conv_ffn(x2, g_in, w_up, conv_w, conv_b, w_down, g_out, layer, seq):
    m, d = x2.shape
    d_ff = w_down.shape[1]
    tm = 512
    f_chunk = 256
    row = pl.BlockSpec((tm, d), lambda i: (i, 0))
    halo = pl.BlockSpec((BF16_ROWS, d), lambda i: (jnp.maximum(i * (tm // BF16_ROWS) - 1, 0), 0))
    single = pl.Buffered(1)
    in_specs = [
        row, halo, _const_spec(g_in.shape),
        pl.BlockSpec((None,) + w_up.shape[1:], lambda i: (layer, 0, 0), pipeline_mode=single),
        _const_spec(conv_w.shape), _const_spec(conv_b.shape),
        pl.BlockSpec((None,) + w_down.shape[1:], lambda i: (layer, 0, 0), pipeline_mode=single),
        _const_spec(g_out.shape),
    ]
    return pl.pallas_call(
        functools.partial(_ffn_kernel, tiles_per_seq=seq // tm, d_ff=d_ff, f_chunk=f_chunk),
        grid=(m // tm,),
        in_specs=in_specs,
        out_specs=row,
        out_shape=jax.ShapeDtypeStruct((m, d), F32),
        scratch_shapes=[pltpu.VMEM((BF16_ROWS + tm, d), BF16),
                        pltpu.VMEM((2, 2, BF16_ROWS + tm, f_chunk), F32),
                        pltpu.VMEM((2, tm, f_chunk), BF16),
                        pltpu.VMEM((tm, d), F32)],
        compiler_params=pltpu.CompilerParams(
            dimension_semantics=("arbitrary",), vmem_limit_bytes=VMEM_LIMIT),
        name="conv_ffn",
    )(x2, x2, g_in, w_up, conv_w, conv_b, w_down, g_out)


def _attn_kernel(sinks_ref, x_ref, xp_ref, gq_ref, gkv_ref, wq_ref, wkv_ref, wo_ref, go_ref,
                 out_ref, o_ref, *, steps_per_seq):
    blk = xp_ref.shape[0]
    n_q = x_ref.shape[0] // blk
    n_pairs = x_ref.shape[1] // PAIR
    kv_width = KV_HEADS * HEAD_DIM
    first = (pl.program_id(0) % steps_per_seq) == 0

    def unit_norm(t):
        return t * lax.rsqrt(jnp.mean(t * t, axis=-1, keepdims=True) + RMS_EPS)

    x = x_ref[...]
    xn = unit_norm(x)
    q_all = (_dot((xn * gq_ref[...]).astype(BF16), wq_ref[...]) * (HEAD_DIM ** -0.5)).astype(BF16)
    xkv = jnp.concatenate([unit_norm(xp_ref[...]), xn], axis=0) * gkv_ref[...]
    kv = _dot(xkv.astype(BF16), wkv_ref[...]).astype(BF16)

    lane = lax.broadcasted_iota(jnp.int32, (1, PAIR), 1)
    head0 = lane < HEAD_DIM
    row = lax.broadcasted_iota(jnp.int32, (2 * blk, 2 * blk), 0)
    col = lax.broadcasted_iota(jnp.int32, (2 * blk, 2 * blk), 1)
    rel = (row & (blk - 1)) - (col - blk)
    in_window = (rel >= 0) & (rel < WINDOW)
    valid = [in_window & (jnp.logical_not(first) | (col >= blk))] + [in_window] * (n_q - 1)
    top = lax.broadcasted_iota(jnp.int32, (2 * blk, 1), 0) < blk


    def dup(x, odd):
        rolled = pltpu.roll(x, HEAD_DIM, 1)
        return jnp.where(head0, rolled, x) if odd else jnp.where(head0, x, rolled)

    kd, vd = {}, {}
    for b in range(n_q):
        band = kv[b * blk:(b + 2) * blk]
        for h in range(KV_HEADS):
            kcol = (h // 2) * PAIR
            kd[b, h] = dup(band[:, kcol:kcol + PAIR], h % 2 == 1)
            vd[b, h] = dup(band[:, kv_width + kcol:kv_width + kcol + PAIR], h % 2 == 1)
    units = [(b, p) for b in range(n_q) for p in range(n_pairs)]

    def qstack(b, p):
        q = q_all[b * blk:(b + 1) * blk, PAIR * p:PAIR * (p + 1)]
        zero = jnp.zeros_like(q)
        return jnp.concatenate([jnp.where(head0, q, zero), jnp.where(head0, zero, q)], axis=0)

    s = [jnp.where(valid[b], _dot_nt(qstack(b, p), kd[b, (2 * p) // GROUP]), NEG_INF)
         for b, p in units]
    sink = [jnp.where(top, sinks_ref[2 * p], sinks_ref[2 * p + 1]) for _, p in units]
    mx = [jnp.maximum(jnp.max(s_, axis=-1, keepdims=True), k_) for s_, k_ in zip(s, sink)]
    pr = [jnp.exp(s_ - m_) for s_, m_ in zip(s, mx)]
    denom = [jnp.sum(p_, axis=-1, keepdims=True) + jnp.exp(k_ - m_)
             for p_, k_, m_ in zip(pr, sink, mx)]
    o2 = [_dot(pr[u].astype(BF16), vd[b, (2 * p) // GROUP]) / denom[u]
          for u, (b, p) in enumerate(units)]
    for u, (b, p) in enumerate(units):
        o_ref[b * blk:(b + 1) * blk, PAIR * p:PAIR * (p + 1)] = jnp.where(
            head0, o2[u][:blk], o2[u][blk:]).astype(o_ref.dtype)

    out_ref[...] = x + _rms(_dot(o_ref[...], wo_ref[...]), go_ref[...])


def _attention(x2, sinks, g_q, g_kv, w_q, w_kv, w_o, g_o, seq):
    m, d = x2.shape
    blk = WINDOW
    n_q = 2
    rows = n_q * blk
    row = pl.BlockSpec((rows, d), lambda i: (i, 0))
    single = pl.Buffered(1)
    weights = [pl.BlockSpec(w.shape, lambda i: (0, 0), pipeline_mode=single)
               for w in (w_q, w_kv, w_o)]
    return pl.pallas_call(
        functools.partial(_attn_kernel, steps_per_seq=seq // rows),
        grid=(m // rows,),
        in_specs=[pl.BlockSpec(memory_space=pltpu.SMEM), row,
                  pl.BlockSpec((blk, d), lambda i: (jnp.maximum(i * n_q - 1, 0), 0)),
                  _const_spec(g_q.shape), _const_spec(g_kv.shape)] + weights
        + [_const_spec(g_o.shape)],
        out_specs=row,
        out_shape=jax.ShapeDtypeStruct((m, d), F32),
        scratch_shapes=[pltpu.VMEM((rows, d), BF16)],
        compiler_params=pltpu.CompilerParams(
            dimension_semantics=("arbitrary",), vmem_limit_bytes=VMEM_LIMIT),
        name="swa_attention",
    )(sinks, x2, x2, g_q, g_kv, w_q, w_kv, w_o, g_o)


def kernel(x, norm_g, mu, w_rkv, w_decay0, w_decay1, w_decay2, a0, a1, a2, g1, g2, k_k, k_a, r_k, gn_g, gn_b, w_o_rwkv, kv_g, w_kv, w_q, sinks, w_o_attn, w_up, conv_w, conv_b, w_down):
    batch, seq, d = x.shape
    n_rwkv = mu.shape[0]
    depth = norm_g.shape[0]
    bf = lambda t: t.astype(BF16)
    vec = lambda t: t.reshape(1, -1)
    x2 = x.reshape(batch * seq, d)
    w_up_b, w_down_b = bf(w_up), bf(w_down)
    for l in range(depth):
        if l < n_rwkv:
            i = l
            parts = _rwkv_pre(
                x2, vec(norm_g[l, 0]), mu[i], bf(w_rkv), i,
                vec(w_decay0[i]), bf(w_decay1[i]), bf(w_decay2[i]),
                vec(a0[i]), bf(a1[i]), bf(a2[i]), bf(g1[i]), bf(g2[i]),
                vec(k_k[i]), vec(k_a[i]), seq)
            x2 = _rwkv_scan(*parts, vec(r_k[i]), vec(gn_g[i]), vec(gn_b[i]),
                            x2, bf(w_o_rwkv[i]), vec(norm_g[l, 1]), batch, seq)
        else:
            assert l == n_rwkv, "one attention layer supported"
            j = l - n_rwkv
            x2 = _attention(x2, sinks[j], vec(norm_g[l, 0]), vec(kv_g), bf(w_q[j]), bf(w_kv),
                            bf(w_o_attn[j]), vec(norm_g[l, 1]), seq)
        x2 = _conv_ffn(x2, vec(norm_g[l, 2]), w_up_b, conv_w[l], vec(conv_b[l]),
                       w_down_b, vec(norm_g[l, 3]), l, seq)
    return x2.reshape(batch, seq, d)
```
